```python
import jax, jax.numpy as jnp
from jax import lax
import numpy as np

D_MODEL = 4096
BATCH = 4
SEQ = 4096
DEPTH = 4

N_MIXERS = 4
HEAD_DIM = 128
N_HEADS = D_MODEL // HEAD_DIM
ROPE_THETA = 500000.0
ROPE_DIM = HEAD_DIM // 4
NORM_EPS = 1e-6
BIG = 1e30
NSA_KV_GROUPS = 4
NSA_HPG = N_HEADS // NSA_KV_GROUPS
CMP_BLOCK = 32
CMP_STRIDE = 16
SEL_BLOCK = 64
SEL_TOPK = 16
WINDOW = 512
NSA_QBLOCK = 32
NSA_IN = D_MODEL + 6 * NSA_KV_GROUPS * HEAD_DIM + 3 * N_HEADS
CONV_WIDTH = 31
CONV_IN = 2 * D_MODEL
SB_QBLOCK = 128
SB_IN = 3 * D_MODEL
GM_CHUNK = 128
GM_GROUP_CH = 128
GM_GROUPS = D_MODEL // GM_GROUP_CH
GM_IN = 2 * D_MODEL
MEM_LEN = 256
MEM_HEADS = 4
MEM_WIDTH = D_MODEL // 4
MEM_HEAD_DIM = MEM_WIDTH // MEM_HEADS
OUT_IN = D_MODEL + MEM_WIDTH
D_FF = (43 * D_MODEL) // 32

kernel_name = "hybrid_nsa_conv_stickbreak_gmlp_decoder"


def rms_norm(x, g):
    xf = x.astype(jnp.float32)
    y = xf * lax.rsqrt(jnp.mean(xf * xf, axis=-1, keepdims=True) + NORM_EPS)
    return (y * g.astype(jnp.float32)).astype(x.dtype)


def layer_norm(x, g, b):
    xf = x.astype(jnp.float32)
    xc = xf - jnp.mean(xf, axis=-1, keepdims=True)
    y = xc * lax.rsqrt(jnp.mean(xc * xc, axis=-1, keepdims=True) + NORM_EPS)
    return (y * g.astype(jnp.float32) + b.astype(jnp.float32)).astype(x.dtype)


def masked_softmax(s, mask):
    p = jax.nn.softmax(jnp.where(mask, s, -BIG), axis=-1)
    return p * mask


def rope_tables(positions):
    inv = 1.0 / (ROPE_THETA ** (jnp.arange(0, ROPE_DIM, 2, dtype=jnp.float32) / ROPE_DIM))
    ang = positions.astype(jnp.float32)[..., None] * inv
    return jnp.cos(ang), jnp.sin(ang)


def apply_partial_rope(x, cos, sin):
    half = ROPE_DIM // 2
    x1, x2, rest = x[..., :half], x[..., half:ROPE_DIM], x[..., ROPE_DIM:]
    c = cos[:, :, None, :].astype(x.dtype)
    s = sin[:, :, None, :].astype(x.dtype)
    return jnp.concatenate([x1 * c - x2 * s, x2 * c + x1 * s, rest], axis=-1)


def swiglu(h, w_gu, w_down):
    gu = h @ w_gu
    g, u = gu[..., :D_FF], gu[..., D_FF:]
    return (jax.nn.silu(g) * u) @ w_down


def nsa_compress(x, pos_emb, w1, w2):
    B, T, G, dh = x.shape
    r = CMP_BLOCK // CMP_STRIDE
    n_pieces = T // CMP_STRIDE
    n_cmp = n_pieces - r + 1
    pieces = x.reshape(B, n_pieces, CMP_STRIDE, G, dh)
    blocks = jnp.concatenate([pieces[:, k:k + n_cmp] for k in range(r)], axis=2)
    blocks = blocks + pos_emb[:, None, :]
    flat = jnp.moveaxis(blocks, 2, 3).reshape(B, n_cmp, G, CMP_BLOCK * dh)
    return jax.nn.silu(flat @ w1) @ w2


def nsa_mixer(p, cos, sin, q_norm, k_norm, cmp_pos, cmp_w1, cmp_w2):
    B, T, _ = p.shape
    G, J, dh = NSA_KV_GROUPS, NSA_HPG, HEAD_DIM
    kv = G * dh
    q = p[..., :D_MODEL].reshape(B, T, N_HEADS, dh)
    kc, vc, ks, vs, kw, vw = [p[..., D_MODEL + i * kv:D_MODEL + (i + 1) * kv].reshape(B, T, G, dh) for i in range(6)]
    gates = jax.nn.sigmoid(p[..., D_MODEL + 6 * kv:]).reshape(B, T, N_HEADS, 3)
    q = apply_partial_rope(rms_norm(q, q_norm), cos, sin)
    kc = apply_partial_rope(rms_norm(kc, k_norm[0]), cos, sin)
    ks = apply_partial_rope(rms_norm(ks, k_norm[1]), cos, sin)
    kw = apply_partial_rope(rms_norm(kw, k_norm[2]), cos, sin)
    k_cmp = nsa_compress(kc, cmp_pos[0], cmp_w1[0], cmp_w2[0])
    v_cmp = nsa_compress(vc, cmp_pos[1], cmp_w1[1], cmp_w2[1])
    n_cmp = k_cmp.shape[1]
    cmp_end = jnp.arange(n_cmp) * CMP_STRIDE + (CMP_BLOCK - 1)
    n_sel = T // SEL_BLOCK
    n_top = min(SEL_TOPK, n_sel)
    rs = SEL_BLOCK // CMP_STRIDE
    r = CMP_BLOCK // CMP_STRIDE
    k_blk = ks.reshape(B, n_sel, SEL_BLOCK, G, dh).transpose(0, 3, 1, 2, 4)
    v_blk = vs.reshape(B, n_sel, SEL_BLOCK, G, dh).transpose(0, 3, 1, 2, 4)
    blk_start = jnp.arange(n_sel) * SEL_BLOCK
    jj = jnp.arange(n_sel)
    b_idx = jnp.arange(B)[:, None, None]
    g_idx = jnp.arange(G)[None, :, None]
    kw_pad = jnp.pad(kw, ((0, 0), (WINDOW, 0), (0, 0), (0, 0)))
    vw_pad = jnp.pad(vw, ((0, 0), (WINDOW, 0), (0, 0), (0, 0)))
    scale = dh ** -0.5
    QB = NSA_QBLOCK

    def block(bi):
        q0 = bi * QB
        t = q0 + jnp.arange(QB)
        qb = lax.dynamic_slice_in_dim(q, q0, QB, axis=1).reshape(B, QB, G, J, dh)
        gb = lax.dynamic_slice_in_dim(gates, q0, QB, axis=1).reshape(B, QB, G, J, 3)
        s_c = jnp.einsum('bqgjd,bngd->bqgjn', qb, k_cmp).astype(jnp.float32) * scale
        p_c = masked_softmax(s_c, (cmp_end[None, :] <= t[:, None])[None, :, None, None, :])
        o_c = jnp.einsum('bqgjn,bngd->bqgjd', p_c.astype(v_cmp.dtype), v_cmp)
        imp = jnp.pad(p_c.sum(axis=3), ((0, 0), (0, 0), (0, 0), (0, n_sel * rs - n_cmp)))
        grp = imp.reshape(B, QB, G, n_sel, rs)
        imp_sel = grp.sum(axis=-1)
        for k in range(1, r):
            imp_sel = imp_sel + jnp.pad(grp[..., rs - k], ((0, 0), (0, 0), (0, 0), (1, 0)))[..., :-1]
        cur = t // SEL_BLOCK
        forced = (jj[None, :] == 0) | (jj[None, :] == cur[:, None]) | (jj[None, :] == cur[:, None] - 1)
        valid = blk_start[None, :] <= t[:, None]
        score = jnp.where(forced[None, :, None, :], BIG, imp_sel)
        score = jnp.where(valid[None, :, None, :], score, -BIG)
        _, top = lax.top_k(score, n_top)
        top_g = top.transpose(0, 2, 1, 3).reshape(B, G, QB * n_top)
        k_sel = k_blk[b_idx, g_idx, top_g].reshape(B, G, QB, n_top * SEL_BLOCK, dh)
        v_sel = v_blk[b_idx, g_idx, top_g].reshape(B, G, QB, n_top * SEL_BLOCK, dh)
        s_s = jnp.einsum('bqgjd,bgqmd->bqgjm', qb, k_sel).astype(jnp.float32) * scale
        tok = (top[..., None] * SEL_BLOCK + jnp.arange(SEL_BLOCK)).reshape(B, QB, G, n_top * SEL_BLOCK)
        p_s = masked_softmax(s_s, (tok <= t[None, :, None, None])[:, :, :, None, :])
        o_s = jnp.einsum('bqgjm,bgqmd->bqgjd', p_s.astype(v_sel.dtype), v_sel)
        kwin = lax.dynamic_slice_in_dim(kw_pad, q0, WINDOW + QB, axis=1)
        vwin = lax.dynamic_slice_in_dim(vw_pad, q0, WINDOW + QB, axis=1)
        kpos = q0 - WINDOW + jnp.arange(WINDOW + QB)
        diff = t[:, None] - kpos[None, :]
        m_w = (kpos[None, :] >= 0) & (diff >= 0) & (diff < WINDOW)
        s_w = jnp.einsum('bqgjd,bkgd->bqgjk', qb, kwin).astype(jnp.float32) * scale
        p_w = masked_softmax(s_w, m_w[None, :, None, None, :])
        o_w = jnp.einsum('bqgjk,bkgd->bqgjd', p_w.astype(vwin.dtype), vwin)
        o = gb[..., 0:1] * o_c + gb[..., 1:2] * o_s + gb[..., 2:3] * o_w
        return o.reshape(B, QB, N_HEADS * dh)

    out = lax.map(block, jnp.arange(T // QB))
    return out.transpose(1, 0, 2, 3).reshape(B, T, N_HEADS * dh)


def conformer_conv_mixer(p, b_in, dw_w, dw_b, ln_g, ln_b):
    p = p + b_in
    u = p[..., :D_MODEL] * jax.nn.sigmoid(p[..., D_MODEL:])
    y = lax.conv_general_dilated(u, dw_w[:, None, :], window_strides=(1,), padding=[(CONV_WIDTH - 1, 0)],
                                 dimension_numbers=('NWC', 'WIO', 'NWC'), feature_group_count=D_MODEL)
    y = layer_norm(y + dw_b, ln_g, ln_b)
    return jax.nn.silu(y)


def stick_breaking_mixer(p):
    B, T, _ = p.shape
    q = p[..., :D_MODEL].reshape(B, T, N_HEADS, HEAD_DIM)
    k = p[..., D_MODEL:2 * D_MODEL].reshape(B, T, N_HEADS, HEAD_DIM)
    v = p[..., 2 * D_MODEL:].reshape(B, T, N_HEADS, HEAD_DIM)
    scale = HEAD_DIM ** -0.5
    s_pos = jnp.arange(T)

    def block(bi):
        q0 = bi * SB_QBLOCK
        t = q0 + jnp.arange(SB_QBLOCK)
        qb = lax.dynamic_slice_in_dim(q, q0, SB_QBLOCK, axis=1)
        z = jnp.einsum('bqhd,bkhd->bhqk', qb, k).astype(jnp.float32) * scale
        strict = (s_pos[None, :] < t[:, None])[None, None]
        log_beta = jax.nn.log_sigmoid(z)
        log_keep = jnp.where(strict, jax.nn.log_sigmoid(-z), 0.0)
        after = lax.cumsum(log_keep, axis=3, reverse=True) - log_keep
        a = jnp.where(strict, jnp.exp(log_beta + after), 0.0)
        o = jnp.einsum('bhqk,bkhd->bqhd', a.astype(v.dtype), v)
        return o.reshape(B, SB_QBLOCK, D_MODEL)

    out = lax.map(block, jnp.arange(T // SB_QBLOCK))
    return out.transpose(1, 0, 2, 3).reshape(B, T, D_MODEL)


def gmlp_mixer(p, ln_g, ln_b, w_s, b_s):
    B, T, _ = p.shape
    z = jax.nn.gelu(p, approximate=False)
    u, v = z[..., :D_MODEL], z[..., D_MODEL:]
    v = layer_norm(v, ln_g, ln_b)
    vc = v.reshape(B, T // GM_CHUNK, GM_CHUNK, GM_GROUPS, GM_GROUP_CH)
    causal = jnp.tril(jnp.ones((GM_CHUNK, GM_CHUNK), dtype=bool))
    ws = jnp.where(causal, w_s, 0)
    mixed = jnp.einsum('gts,bnsgc->bntgc', ws, vc) + b_s.T[None, None, :, :, None]
    return u * mixed.reshape(B, T, D_MODEL)


def memory_attention(mq, mk, mv, q_norm):
    B, T = mq.shape[0], mq.shape[1]
    mq = rms_norm(mq, q_norm)
    s = jnp.einsum('bthd,bmhd->bhtm', mq, mk).astype(jnp.float32) * (MEM_HEAD_DIM ** -0.5)
    pr = jax.nn.softmax(s, axis=-1)
    o = jnp.einsum('bhtm,bmhd->bthd', pr.astype(mv.dtype), mv)
    return o.reshape(B, T, MEM_WIDTH)


def setup_inputs(seed: int = 0) -> dict:
    key = jax.random.key(seed)
    ks = iter(jax.random.split(key, 48))
    f32 = jnp.float32

    def nrm(shape, fan_in):
        return jax.random.normal(next(ks), shape, f32) * (fan_in ** -0.5)

    def gain(shape):
        return 1.0 + 0.02 * jax.random.normal(next(ks), shape, f32)

    def small(shape, s=0.02):
        return s * jax.random.normal(next(ks), shape, f32)

    nA = len(range(0, DEPTH, N_MIXERS))
    nB = len(range(1, DEPTH, N_MIXERS))
    nC = len(range(2, DEPTH, N_MIXERS))
    nD = len(range(3, DEPTH, N_MIXERS))
    kv = NSA_KV_GROUPS * HEAD_DIM
    return {
        "x": jax.random.normal(next(ks), (BATCH, SEQ, D_MODEL), f32),
        "mem": jax.random.normal(next(ks), (BATCH, MEM_LEN, D_MODEL), f32),
        "positions": jax.random.randint(next(ks), (BATCH, 1), 0, 1024, dtype=jnp.int32) + jnp.arange(SEQ, dtype=jnp.int32)[None, :],
        "ffn1_norm": gain((DEPTH, D_MODEL)),
        "ffn1_w_gu": nrm((DEPTH, D_MODEL, 2 * D_FF), D_MODEL),
        "ffn1_w_down": nrm((DEPTH, D_FF, D_MODEL), D_FF),
        "mix_norm": gain((DEPTH, D_MODEL)),
        "ffn2_norm": gain((DEPTH, D_MODEL)),
        "ffn2_w_gu": nrm((DEPTH, D_MODEL, 2 * D_FF), D_MODEL),
        "ffn2_w_down": nrm((DEPTH, D_FF, D_MODEL), D_FF),
        "mem_norm": gain((D_MODEL,)),
        "mem_w_kv": nrm((D_MODEL, 2 * MEM_WIDTH), D_MODEL),
        "mem_k_norm": gain((MEM_HEAD_DIM,)),
        "mem_q_norm": gain((DEPTH, MEM_HEAD_DIM)),
        "nsa_w_in": nrm((nA, D_MODEL, NSA_IN + MEM_WIDTH), D_MODEL),
        "nsa_q_norm": gain((nA, HEAD_DIM)),
        "nsa_k_norm": gain((nA, 3, HEAD_DIM)),
        "nsa_cmp_pos": small((nA, 2, CMP_BLOCK, HEAD_DIM), 0.2),
        "nsa_cmp_w1": nrm((nA, 2, CMP_BLOCK * HEAD_DIM, HEAD_DIM), CMP_BLOCK * HEAD_DIM),
        "nsa_cmp_w2": nrm((nA, 2, HEAD_DIM, HEAD_DIM), HEAD_DIM),
        "nsa_w_out": nrm((nA, OUT_IN, D_MODEL), OUT_IN),
        "conv_w_in": nrm((nB, D_MODEL, CONV_IN + MEM_WIDTH), D_MODEL),
        "conv_b_in": small((nB, CONV_IN)),
        "conv_dw_w": nrm((nB, CONV_WIDTH, D_MODEL), CONV_WIDTH),
        "conv_dw_b": small((nB, D_MODEL)),
        "conv_ln_g": gain((nB, D_MODEL)),
        "conv_ln_b": small((nB, D_MODEL)),
        "conv_w_out": nrm((nB, OUT_IN, D_MODEL), OUT_IN),
        "sb_w_in": nrm((nC, D_MODEL, SB_IN + MEM_WIDTH), D_MODEL),
        "sb_w_out": nrm((nC, OUT_IN, D_MODEL), OUT_IN),
        "gm_w_in": nrm((nD, D_MODEL, GM_IN + MEM_WIDTH), D_MODEL),
        "gm_ln_g": gain((nD, D_MODEL)),
        "gm_ln_b": small((nD, D_MODEL)),
        "gm_ws": nrm((nD, GM_GROUPS, GM_CHUNK, GM_CHUNK), GM_CHUNK),
        "gm_bs": gain((nD, GM_GROUPS, GM_CHUNK)),
        "gm_w_out": nrm((nD, OUT_IN, D_MODEL), OUT_IN),
    }


def reference(x, mem, positions, ffn1_norm, ffn1_w_gu, ffn1_w_down, mix_norm, ffn2_norm, ffn2_w_gu, ffn2_w_down,
              mem_norm, mem_w_kv, mem_k_norm, mem_q_norm,
              nsa_w_in, nsa_q_norm, nsa_k_norm, nsa_cmp_pos, nsa_cmp_w1, nsa_cmp_w2, nsa_w_out,
              conv_w_in, conv_b_in, conv_dw_w, conv_dw_b, conv_ln_g, conv_ln_b, conv_w_out,
              sb_w_in, sb_w_out, gm_w_in, gm_ln_g, gm_ln_b, gm_ws, gm_bs, gm_w_out):
    B, T, _ = x.shape
    cos, sin = rope_tables(positions)
    mkv = rms_norm(mem, mem_norm) @ mem_w_kv
    mk = rms_norm(mkv[..., :MEM_WIDTH].reshape(B, MEM_LEN, MEM_HEADS, MEM_HEAD_DIM), mem_k_norm)
    mv = mkv[..., MEM_WIDTH:].reshape(B, MEM_LEN, MEM_HEADS, MEM_HEAD_DIM)
    for i in range(DEPTH):
        kind, j = i % N_MIXERS, i // N_MIXERS
        x = x + 0.5 * swiglu(rms_norm(x, ffn1_norm[i]), ffn1_w_gu[i], ffn1_w_down[i])
        h = rms_norm(x, mix_norm[i])
        if kind == 0:
            proj = h @ nsa_w_in[j]
            mixed = nsa_mixer(proj[..., :-MEM_WIDTH], cos, sin, nsa_q_norm[j], nsa_k_norm[j],
                              nsa_cmp_pos[j], nsa_cmp_w1[j], nsa_cmp_w2[j])
            w_out = nsa_w_out[j]
        elif kind == 1:
            proj = h @ conv_w_in[j]
            mixed = conformer_conv_mixer(proj[..., :-MEM_WIDTH], conv_b_in[j], conv_dw_w[j], conv_dw_b[j],
                                         conv_ln_g[j], conv_ln_b[j])
            w_out = conv_w_out[j]
        elif kind == 2:
            proj = h @ sb_w_in[j]
            mixed = stick_breaking_mixer(proj[..., :-MEM_WIDTH])
            w_out = sb_w_out[j]
        else:
            proj = h @ gm_w_in[j]
            mixed = gmlp_mixer(proj[..., :-MEM_WIDTH], gm_ln_g[j], gm_ln_b[j], gm_ws[j], gm_bs[j])
            w_out = gm_w_out[j]
        mq = proj[..., -MEM_WIDTH:].reshape(B, T, MEM_HEADS, MEM_HEAD_DIM)
        mem_out = memory_attention(mq, mk, mv, mem_q_norm[i])
        x = x + jnp.concatenate([mixed, mem_out], axis=-1) @ w_out
        x = x + 0.5 * swiglu(rms_norm(x, ffn2_norm[i]), ffn2_w_gu[i], ffn2_w_down[i])
    return x
```

```python
import functools

import jax
import jax.numpy as jnp
from jax import lax
from jax.experimental import pallas as pl
from jax.experimental.pallas import tpu as pltpu

F32 = jnp.float32
BF16 = jnp.bfloat16

HEAD_DIM = 128
ROPE_THETA = 500000.0
ROPE_DIM = HEAD_DIM // 4
NORM_EPS = 1e-6
BIG = 1e30
NSA_KV_GROUPS = 4
CMP_BLOCK = 32
CMP_STRIDE = 16
SEL_BLOCK = 64
SEL_TOPK = 16
WINDOW = 512
CONV_WIDTH = 31
GM_CHUNK = 128
MEM_HEADS = 4

LANES = 128
MXU_WIDTH = 256
VMEM_LIMIT_BYTES = 56 * 1024 * 1024
SEL_MASK_BIAS = -32768.0


def _cparams(*sem):
    return pltpu.CompilerParams(dimension_semantics=sem, vmem_limit_bytes=VMEM_LIMIT_BYTES)


def _dot(a, b):
    return jnp.dot(a, b, preferred_element_type=F32)


def _dot_nt(a, b):
    return lax.dot_general(a, b, (((1,), (1,)), ((), ())), preferred_element_type=F32)


def _floordiv_pow2(x, n):
    return lax.shift_right_logical(x, n.bit_length() - 1)


def _rms(x, g):
    return x * lax.rsqrt(jnp.mean(x * x, axis=-1, keepdims=True) + NORM_EPS) * g


def _norm_matmul_body(x_ref, g_ref, w_ref, o_ref, h_ref):
    @pl.when(pl.program_id(1) == 0)
    def _():
        h_ref[...] = _rms(x_ref[...], g_ref[...]).astype(BF16)

    o_ref[...] = _dot(h_ref[...], w_ref[...]).astype(o_ref.dtype)


def _norm_matmul_side_body(x_ref, g_ref, w_ref, ws_ref, o_ref, os_ref, h_ref):
    @pl.when(pl.program_id(1) == 0)
    def _():
        h = _rms(x_ref[...], g_ref[...]).astype(BF16)
        h_ref[...] = h
        os_ref[...] = jax.nn.sigmoid(_dot(h, ws_ref[...]))

    o_ref[...] = _dot(h_ref[...], w_ref[...]).astype(o_ref.dtype)


def norm_matmul(x, g, w, *, tm=512, tn=512, side_w=None):
    m, k = x.shape
    n = w.shape[1]
    tm = min(tm, m)
    grid = (m // tm, n // tn)
    x_spec = pl.BlockSpec((tm, k), lambda i, j: (i, 0))
    g_spec = pl.BlockSpec((1, k), lambda i, j: (0, 0))
    w_spec = pl.BlockSpec((k, tn), lambda i, j: (0, j))
    o_spec = pl.BlockSpec((tm, tn), lambda i, j: (i, j))
    scratch = [pltpu.VMEM((tm, k), BF16)]
    if side_w is None:
        return pl.pallas_call(
            _norm_matmul_body, grid=grid, in_specs=[x_spec, g_spec, w_spec], out_specs=o_spec,
            out_shape=jax.ShapeDtypeStruct((m, n), BF16), scratch_shapes=scratch,
            compiler_params=_cparams("parallel", "arbitrary"), name="norm_matmul")(x, g.reshape(1, k), w)
    ns = side_w.shape[1]
    return pl.pallas_call(
        _norm_matmul_side_body, grid=grid,
        in_specs=[x_spec, g_spec, w_spec, pl.BlockSpec((k, ns), lambda i, j: (0, 0))],
        out_specs=[o_spec, pl.BlockSpec((tm, ns), lambda i, j: (i, 0))],
        out_shape=[jax.ShapeDtypeStruct((m, n), BF16), jax.ShapeDtypeStruct((m, ns), F32)],
        scratch_shapes=scratch,
        compiler_params=_cparams("parallel", "arbitrary"), name="norm_matmul_side")(x, g.reshape(1, k), w, side_w)


def _norm_swiglu_body(x_ref, g_ref, wg_ref, wu_ref, o_ref, h_ref):
    @pl.when(pl.program_id(1) == 0)
    def _():
        h_ref[...] = _rms(x_ref[...], g_ref[...]).astype(BF16)

    h = h_ref[...]
    gate = _dot(h, wg_ref[...])
    up = _dot(h, wu_ref[...])
    o_ref[...] = (gate * jax.nn.sigmoid(gate) * up).astype(o_ref.dtype)


def norm_swiglu(x, g, wg, wu, *, tm=512, tn=512):
    m, k = x.shape
    n = wg.shape[1]
    tm = min(tm, m)
    grid = (m // tm, n // tn)
    return pl.pallas_call(
        _norm_swiglu_body, grid=grid,
        in_specs=[pl.BlockSpec((tm, k), lambda i, j: (i, 0)),
                  pl.BlockSpec((1, k), lambda i, j: (0, 0)),
                  pl.BlockSpec((k, tn), lambda i, j: (0, j)),
                  pl.BlockSpec((k, tn), lambda i, j: (0, j))],
        out_specs=pl.BlockSpec((tm, tn), lambda i, j: (i, j)),
        out_shape=jax.ShapeDtypeStruct((m, n), BF16),
        scratch_shapes=[pltpu.VMEM((tm, k), BF16)],
        compiler_params=_cparams("parallel", "arbitrary"), name="norm_swiglu")(x, g.reshape(1, k), wg, wu)


def _matmul_residual_body(scale, a_ref, w_ref, x_ref, o_ref):
    o_ref[...] = x_ref[...] + scale * _dot(a_ref[...], w_ref[...])


def matmul_residual(a, w, x, scale, *, tm=1024, tn=512):
    m, k = a.shape
    n = w.shape[1]
    tm = min(tm, m)
    return pl.pallas_call(
        functools.partial(_matmul_residual_body, scale), grid=(m // tm, n // tn),
        in_specs=[pl.BlockSpec((tm, k), lambda i, j: (i, 0)),
                  pl.BlockSpec((k, tn), lambda i, j: (0, j)),
                  pl.BlockSpec((tm, tn), lambda i, j: (i, j))],
        out_specs=pl.BlockSpec((tm, tn), lambda i, j: (i, j)),
        out_shape=jax.ShapeDtypeStruct((m, n), F32),
        compiler_params=_cparams("parallel", "arbitrary"), name="matmul_residual")(a, w, x)


def _matmul2_residual_body(a_ref, b_ref, wa_ref, wb_ref, x_ref, o_ref):
    o_ref[...] = x_ref[...] + _dot(a_ref[...], wa_ref[...]) + _dot(b_ref[...], wb_ref[...])


def matmul2_residual(a, b, wa, wb, x, *, tm=1024, tn=512):
    m, ka = a.shape
    kb = b.shape[1]
    n = wa.shape[1]
    tm = min(tm, m)
    return pl.pallas_call(
        _matmul2_residual_body, grid=(m // tm, n // tn),
        in_specs=[pl.BlockSpec((tm, ka), lambda i, j: (i, 0)),
                  pl.BlockSpec((tm, kb), lambda i, j: (i, 0)),
                  pl.BlockSpec((ka, tn), lambda i, j: (0, j)),
                  pl.BlockSpec((kb, tn), lambda i, j: (0, j)),
                  pl.BlockSpec((tm, tn), lambda i, j: (i, j))],
        out_specs=pl.BlockSpec((tm, tn), lambda i, j: (i, j)),
        out_shape=jax.ShapeDtypeStruct((m, n), F32),
        compiler_params=_cparams("parallel", "arbitrary"), name="matmul2_residual")(a, b, wa, wb, x)


def _mem_attn_body(heads, mq_ref, mkv_ref, qn_ref, kn_ref, o_ref):
    width = mq_ref.shape[-1]
    dh = width // heads
    scale = dh ** -0.5
    for h in range(heads):
        q = _rms(mq_ref[:, h * dh:(h + 1) * dh].astype(F32), qn_ref[...]) * scale
        k = _rms(mkv_ref[:, h * dh:(h + 1) * dh].astype(F32), kn_ref[...])
        v = mkv_ref[:, width + h * dh:width + (h + 1) * dh]
        s = _dot_nt(q.astype(BF16), k.astype(BF16))
        p = jnp.exp(s - jnp.max(s, axis=-1, keepdims=True))
        p = p / jnp.sum(p, axis=-1, keepdims=True)
        o_ref[:, h * dh:(h + 1) * dh] = _dot(p.astype(BF16), v).astype(o_ref.dtype)


def mem_attention(proj, mkv, q_norm, k_norm, *, width, tm=512):
    b, t, c = proj.shape
    mlen = mkv.shape[1]
    dh = width // MEM_HEADS
    qblk = (c - width) // width
    return pl.pallas_call(
        functools.partial(_mem_attn_body, MEM_HEADS), grid=(b, t // tm),
        in_specs=[pl.BlockSpec((None, tm, width), lambda bi, i: (bi, i, qblk)),
                  pl.BlockSpec((None, mlen, 2 * width), lambda bi, i: (bi, 0, 0)),
                  pl.BlockSpec((1, dh), lambda bi, i: (0, 0)),
                  pl.BlockSpec((1, dh), lambda bi, i: (0, 0))],
        out_specs=pl.BlockSpec((None, tm, width), lambda bi, i: (bi, i, 0)),
        out_shape=jax.ShapeDtypeStruct((b, t, width), BF16),
        compiler_params=_cparams("parallel", "parallel"), name="mem_attention",
    )(proj, mkv, q_norm.reshape(1, dh), k_norm.reshape(1, dh))


def _rope(y, c, s1, s2):
    half = ROPE_DIM // 2
    return y * c + pltpu.roll(y, HEAD_DIM - half, 1) * s1 + pltpu.roll(y, half, 1) * s2


def _nsa_prep_body(groups, q_ref, cmp_ref, sel_ref, win_ref, c_ref, s1_ref, s2_ref, qn_ref, kn_ref,
                   qo_ref, kcv_ref, ksa_ref, kw_ref):
    dh = HEAD_DIM
    tt = q_ref.shape[0]
    n_heads = q_ref.shape[1] // dh
    c, s1, s2 = c_ref[...], s1_ref[...], s2_ref[...]
    scale = dh ** -0.5
    qn = qn_ref[...] * scale
    for h in range(n_heads):
        y = _rms(q_ref[:, h * dh:(h + 1) * dh].astype(F32), qn)
        qo_ref[:, h * dh:(h + 1) * dh] = _rope(y, c, s1, s2).astype(BF16)
    t = pl.program_id(1) * tt + lax.broadcasted_iota(jnp.int32, (tt, dh), 0)
    lane = lax.broadcasted_iota(jnp.int32, (tt, dh), 1)
    onehot = jnp.where(lane == _floordiv_pow2(t, SEL_BLOCK), 1.0, 0.0).astype(BF16)
    for g in range(groups):
        sl = slice(g * dh, (g + 1) * dh)
        kc = _rope(_rms(cmp_ref[:, sl].astype(F32), kn_ref[0:1, :]), c, s1, s2)
        kcv_ref[g] = kc.astype(BF16)
        kcv_ref[groups + g] = cmp_ref[:, groups * dh + g * dh:groups * dh + (g + 1) * dh]
        ks = _rope(_rms(sel_ref[:, sl].astype(F32), kn_ref[1:2, :]), c, s1, s2)
        ksa_ref[:, 2 * g * dh:(2 * g + 1) * dh] = ks.astype(BF16)
        ksa_ref[:, (2 * g + 1) * dh:(2 * g + 2) * dh] = onehot
        kw = _rope(_rms(win_ref[:, sl].astype(F32), kn_ref[2:3, :]), c, s1, s2)
        kw_ref[:, sl] = kw.astype(BF16)


def nsa_prep(proj, rope_c, rope_s1, rope_s2, q_norm, k_norm, *, d_model, tt=256):
    b, t, _ = proj.shape
    g, dh = NSA_KV_GROUPS, HEAD_DIM
    kv2 = 2 * g * dh
    base = d_model // kv2
    tab = pl.BlockSpec((None, tt, dh), lambda bi, i: (bi, i, 0))
    return pl.pallas_call(
        functools.partial(_nsa_prep_body, g), grid=(b, t // tt),
        in_specs=[pl.BlockSpec((None, tt, d_model), lambda bi, i: (bi, i, 0)),
                  pl.BlockSpec((None, tt, kv2), lambda bi, i: (bi, i, base)),
                  pl.BlockSpec((None, tt, kv2), lambda bi, i: (bi, i, base + 1)),
                  pl.BlockSpec((None, tt, kv2), lambda bi, i: (bi, i, base + 2)),
                  tab, tab, tab,
                  pl.BlockSpec((1, dh), lambda bi, i: (0, 0)),
                  pl.BlockSpec((3, dh), lambda bi, i: (0, 0))],
        out_specs=[pl.BlockSpec((None, tt, d_model), lambda bi, i: (bi, i, 0)),
                   pl.BlockSpec((None, 2 * g, tt, dh), lambda bi, i: (bi, 0, i, 0)),
                   pl.BlockSpec((None, tt, 2 * g * dh), lambda bi, i: (bi, i, 0)),
                   pl.BlockSpec((None, tt, g * dh), lambda bi, i: (bi, i, 0))],
        out_shape=[jax.ShapeDtypeStruct((b, t, d_model), BF16),
                   jax.ShapeDtypeStruct((b, 2 * g, t, dh), BF16),
                   jax.ShapeDtypeStruct((b, t, 2 * g * dh), BF16),
                   jax.ShapeDtypeStruct((b, t, g * dh), BF16)],
        compiler_params=_cparams("parallel", "parallel"), name="nsa_prep",
    )(proj, proj, proj, proj, rope_c, rope_s1, rope_s2, q_norm.reshape(1, dh), k_norm)


def _nsa_compress_body(p_ref, w1_ref, w2_ref, pos_ref, o_ref):
    n_pieces, half = p_ref.shape
    p = p_ref[...]
    u = _dot(p, w1_ref[0:half, :])
    v = _dot(p, w1_ref[half:2 * half, :])
    pos = jnp.broadcast_to(pos_ref[...], (8, 2 * half)).astype(BF16)
    c = _dot(pos, w1_ref[...])[0:1, :]
    hid = u + pltpu.roll(v, n_pieces - 1, 0) + c
    hid = hid * jax.nn.sigmoid(hid)
    o_ref[...] = _dot(hid.astype(BF16), w2_ref[...]).astype(o_ref.dtype)


def nsa_compress(kcv, w1, w2, pos):
    b, g2, n_pieces, half = kcv.shape
    dh = HEAD_DIM
    g = g2 // 2
    return pl.pallas_call(
        _nsa_compress_body, grid=(b, g2),
        in_specs=[pl.BlockSpec((None, None, n_pieces, half), lambda bi, gi: (bi, gi, 0, 0)),
                  pl.BlockSpec((None, 2 * half, dh), lambda bi, gi: (gi // g, 0, 0)),
                  pl.BlockSpec((None, dh, dh), lambda bi, gi: (gi // g, 0, 0)),
                  pl.BlockSpec((None, 1, 2 * half), lambda bi, gi: (gi // g, 0, 0))],
        out_specs=pl.BlockSpec((None, None, n_pieces, dh), lambda bi, gi: (bi, gi, 0, 0)),
        out_shape=jax.ShapeDtypeStruct((b, g2, n_pieces, dh), BF16),
        compiler_params=_cparams("parallel", "parallel"), name="nsa_compress",
    )(kcv, w1, w2, pos)


def _softmax_rows(s, mask):
    s = jnp.where(mask, s, -BIG)
    p = jnp.where(mask, jnp.exp(s - jnp.max(s, axis=-1, keepdims=True)), 0.0)
    l = jnp.sum(p, axis=-1, keepdims=True)
    return p * jnp.where(l > 0.0, 1.0 / l, 0.0)


def _nsa_attn_body(tk, q_ref, gate_ref, kc_ref, vc_ref, ksa_ref, vs_ref, kw_ref, vw_ref, o_ref):
    dh = HEAD_DIM
    tq = q_ref.shape[0]
    hpg = q_ref.shape[1] // dh
    n_cmp = kc_ref.shape[0]
    q0 = pl.program_id(2) * tq
    t_col = q0 + lax.broadcasted_iota(jnp.int32, (tq, 1), 0)

    kc, vc = kc_ref[...], vc_ref[...]
    n_row = lax.broadcasted_iota(jnp.int32, (1, n_cmp), 1)
    cmask = (n_row * CMP_STRIDE + (CMP_BLOCK - 1)) <= t_col
    imp = jnp.zeros((tq, n_cmp), F32)
    o_cmp = []
    for j in range(hpg):
        p = _softmax_rows(_dot_nt(q_ref[:, j * dh:(j + 1) * dh], kc), cmask)
        imp = imp + p
        o_cmp.append(_dot(p.astype(BF16), vc))

    rs = SEL_BLOCK // CMP_STRIDE
    nn = lax.broadcasted_iota(jnp.int32, (n_cmp, LANES), 0)
    mm = lax.broadcasted_iota(jnp.int32, (n_cmp, LANES), 1)
    pool = jnp.where((nn >= rs * mm - (CMP_BLOCK // CMP_STRIDE - 1)) & (nn <= rs * mm + rs - 1), 1.0, 0.0).astype(BF16)
    imp_hi = imp.astype(BF16)
    imp_lo = (imp - imp_hi.astype(F32)).astype(BF16)
    imp_sel = _dot(imp_hi, pool) + _dot(imp_lo, pool)

    n_sel = ksa_ref.shape[0] // SEL_BLOCK
    lane = lax.broadcasted_iota(jnp.int32, (tq, LANES), 1)
    cur = _floordiv_pow2(t_col, SEL_BLOCK)
    forced = (lane == 0) | (lane == cur) | (lane == cur - 1)
    valid = (lane <= cur) & (lane < n_sel)
    score = jnp.where(valid, jnp.where(forced, BIG, imp_sel), -BIG)
    rank = jnp.zeros((tq, LANES), jnp.int32)
    for m in range(n_sel):
        col = jnp.broadcast_to(score[:, m:m + 1], (tq, LANES))
        beats = (col > score) | ((col == score) & (lane > m))
        rank = rank + beats.astype(jnp.int32)
    selected = valid & (rank < min(SEL_TOPK, n_sel))
    sel_bias = jnp.where(selected, 0.0, SEL_MASK_BIAS).astype(BF16)

    w_len = WINDOW + tq
    w_start = pl.multiple_of(jnp.maximum(q0 - WINDOW, 0), tq)
    k_win = kw_ref[pl.ds(w_start, w_len), :]
    v_win = vw_ref[pl.ds(w_start, w_len), :]
    diff = t_col - (w_start + lax.broadcasted_iota(jnp.int32, (1, w_len), 1))
    wmask = (diff >= 0) & (diff < WINDOW)

    n_full = q0 // tk
    d_start = pl.multiple_of(n_full * tk, tk)
    dmask = (d_start + lax.broadcasted_iota(jnp.int32, (1, tk), 1)) <= t_col

    for j in range(hpg):
        q = q_ref[:, j * dh:(j + 1) * dh]
        q_aug = jnp.concatenate([q, sel_bias], axis=-1)

        def step(kt, carry):
            m_run, l_run, acc = carry
            ks = pl.multiple_of(kt * tk, tk)
            s = _dot_nt(q_aug, ksa_ref[pl.ds(ks, tk), :])
            m_new = jnp.maximum(m_run, jnp.max(s, axis=-1, keepdims=True))
            alpha = jnp.exp(m_run - m_new)
            p = jnp.exp(s - m_new)
            l_new = alpha * l_run + jnp.sum(p, axis=-1, keepdims=True)
            acc = alpha * acc + _dot(p.astype(BF16), vs_ref[pl.ds(ks, tk), :])
            return m_new, l_new, acc

        init = (jnp.full((tq, 1), -BIG, F32), jnp.zeros((tq, 1), F32), jnp.zeros((tq, dh), F32))
        m_run, l_run, acc = lax.fori_loop(0, n_full, step, init)
        s = jnp.where(dmask, _dot_nt(q_aug, ksa_ref[pl.ds(d_start, tk), :]), -BIG)
        m_new = jnp.maximum(m_run, jnp.max(s, axis=-1, keepdims=True))
        alpha = jnp.exp(m_run - m_new)
        p = jnp.where(dmask, jnp.exp(s - m_new), 0.0)
        l_new = alpha * l_run + jnp.sum(p, axis=-1, keepdims=True)
        acc = alpha * acc + _dot(p.astype(BF16), vs_ref[pl.ds(d_start, tk), :])
        o_sel = acc / l_new

        p = _softmax_rows(_dot_nt(q, k_win), wmask)
        o_win = _dot(p.astype(BF16), v_win)

        g_cmp = gate_ref[:, j:j + 1]
        g_sel = gate_ref[:, hpg + j:hpg + j + 1]
        g_win = gate_ref[:, 2 * hpg + j:2 * hpg + j + 1]
        o_ref[:, j * dh:(j + 1) * dh] = (g_cmp * o_cmp[j] + g_sel * o_sel + g_win * o_win).astype(o_ref.dtype)


def nsa_attention(qn, gates, kv_cmp, ks_aug, kw, proj, *, d_model, tq=128, tk=256):
    b, t, _ = qn.shape
    g, dh = NSA_KV_GROUPS, HEAD_DIM
    hpg = d_model // dh // g
    n_cmp = kv_cmp.shape[2]
    vs_blk = (d_model + 3 * g * dh) // dh
    vw_blk = (d_model + 5 * g * dh) // dh
    return pl.pallas_call(
        functools.partial(_nsa_attn_body, tk), grid=(b, g, t // tq),
        in_specs=[pl.BlockSpec((None, tq, hpg * dh), lambda bi, gi, i: (bi, i, gi)),
                  pl.BlockSpec((None, tq, LANES), lambda bi, gi, i: (bi, i, gi)),
                  pl.BlockSpec((None, None, n_cmp, dh), lambda bi, gi, i: (bi, gi, 0, 0)),
                  pl.BlockSpec((None, None, n_cmp, dh), lambda bi, gi, i: (bi, g + gi, 0, 0)),
                  pl.BlockSpec((None, t, 2 * dh), lambda bi, gi, i: (bi, 0, gi)),
                  pl.BlockSpec((None, t, dh), lambda bi, gi, i: (bi, 0, vs_blk + gi)),
                  pl.BlockSpec((None, t, dh), lambda bi, gi, i: (bi, 0, gi)),
                  pl.BlockSpec((None, t, dh), lambda bi, gi, i: (bi, 0, vw_blk + gi))],
        out_specs=pl.BlockSpec((None, tq, hpg * dh), lambda bi, gi, i: (bi, i, gi)),
        out_shape=jax.ShapeDtypeStruct((b, t, d_model), BF16),
        compiler_params=_cparams("parallel", "parallel", "arbitrary"), name="nsa_attention",
    )(qn, gates, kv_cmp, kv_cmp, ks_aug, proj, kw, proj)


def nsa_mixer(proj, gates, rope_tabs, q_norm, k_norm, cmp_pos, cmp_w1, cmp_w2, *, d_model):
    b, t, _ = proj.shape
    dh = HEAD_DIM
    qn, kcv, ks_aug, kw = nsa_prep(proj, *rope_tabs, q_norm, k_norm, d_model=d_model)
    n_pieces = t // CMP_STRIDE
    kcv = kcv.reshape(b, 2 * NSA_KV_GROUPS, n_pieces, CMP_STRIDE * dh)
    kv_cmp = nsa_compress(kcv, cmp_w1.astype(BF16), cmp_w2.astype(BF16), cmp_pos.reshape(2, 1, CMP_BLOCK * dh))
    return nsa_attention(qn, gates.reshape(b, t, NSA_KV_GROUPS * LANES), kv_cmp, ks_aug, kw, proj, d_model=d_model)


def _conv_body(halo, cw, p_ref, ph_ref, bin_ref, w_ref, wb_ref, lg_ref, lb_ref, o_ref, u_ref, y_ref):
    tt, d = o_ref.shape
    i = pl.program_id(1)

    def glu(ref):
        a = ref[:, 0:d].astype(F32) + bin_ref[:, 0:d]
        gate = ref[:, d:2 * d].astype(F32) + bin_ref[:, d:2 * d]
        return a * jax.nn.sigmoid(gate)

    u_ref[halo:halo + tt, :] = glu(p_ref)

    @pl.when(i == 0)
    def _():
        u_ref[0:halo, :] = jnp.zeros((halo, d), F32)

    @pl.when(i > 0)
    def _():
        u_ref[0:halo, :] = glu(ph_ref)

    off = halo - (CONV_WIDTH - 1)
    for c0 in range(0, d, cw):
        acc = jnp.zeros((tt, cw), F32)
        for k in range(CONV_WIDTH):
            acc = acc + w_ref[k:k + 1, c0:c0 + cw] * u_ref[off + k:off + k + tt, c0:c0 + cw]
        y_ref[:, c0:c0 + cw] = acc + wb_ref[:, c0:c0 + cw]

    y = y_ref[...]
    yc = y - jnp.mean(y, axis=-1, keepdims=True)
    yn = yc * lax.rsqrt(jnp.mean(yc * yc, axis=-1, keepdims=True) + NORM_EPS) * lg_ref[...] + lb_ref[...]
    o_ref[...] = (yn * jax.nn.sigmoid(yn)).astype(o_ref.dtype)


def conv_mixer(proj, b_in, dw_w, dw_b, ln_g, ln_b, *, d_model, tt=256, halo=32, cw=512):
    b, t, _ = proj.shape
    d = d_model
    hb = tt // halo
    row = lambda v: v.reshape(1, -1)
    const = lambda shape: pl.BlockSpec(shape, lambda bi, i: (0, 0))
    return pl.pallas_call(
        functools.partial(_conv_body, halo, cw), grid=(b, t // tt),
        in_specs=[pl.BlockSpec((None, tt, 2 * d), lambda bi, i: (bi, i, 0)),
                  pl.BlockSpec((None, halo, 2 * d), lambda bi, i: (bi, jnp.maximum(i * hb - 1, 0), 0)),
                  const((1, 2 * d)), const((CONV_WIDTH, d)), const((1, d)), const((1, d)), const((1, d))],
        out_specs=pl.BlockSpec((None, tt, d), lambda bi, i: (bi, i, 0)),
        out_shape=jax.ShapeDtypeStruct((b, t, d), BF16),
        scratch_shapes=[pltpu.VMEM((halo + tt, d), F32), pltpu.VMEM((tt, d), F32)],
        compiler_params=_cparams("parallel", "arbitrary"), name="conv_mixer",
    )(proj, proj, row(b_in), dw_w, row(dw_b), row(ln_g), row(ln_b))


def _sb_body(tk, q_ref, k_ref, v_ref, o_ref):
    tq, dh = q_ref.shape
    scale = dh ** -0.5
    q = q_ref[...]
    q0 = pl.program_id(2) * tq
    ss = lax.broadcasted_iota(jnp.int32, (tk, tk), 0)
    jj = lax.broadcasted_iota(jnp.int32, (tk, tk), 1)
    suffix = jnp.where(ss > jj, 1.0, 0.0).astype(BF16)

    def tile(ks, carry, strict):
        c_run, acc = carry
        z = _dot_nt(q, k_ref[pl.ds(ks, tk), :]) * scale
        log_beta = jnp.minimum(z, 0.0) - jnp.log1p(jnp.exp(-jnp.abs(z)))
        log_keep = log_beta - z
        if strict is not None:
            log_keep = jnp.where(strict, log_keep, 0.0)
        hi = log_keep.astype(BF16)
        lo = (log_keep - hi.astype(F32)).astype(BF16)
        after = _dot(hi, suffix) + _dot(lo, suffix) + c_run
        a = jnp.exp(log_beta + after)
        if strict is not None:
            a = jnp.where(strict, a, 0.0)
        acc = acc + _dot(a.astype(BF16), v_ref[pl.ds(ks, tk), :])
        c_run = c_run + jnp.sum(log_keep, axis=-1, keepdims=True)
        return c_run, acc

    carry = (jnp.zeros((tq, 1), F32), jnp.zeros((tq, dh), F32))
    t_col = q0 + lax.broadcasted_iota(jnp.int32, (tq, 1), 0)
    for d in range(tq // tk - 1, -1, -1):
        ks = pl.multiple_of(q0 + d * tk, tk)
        strict = (ks + lax.broadcasted_iota(jnp.int32, (1, tk), 1)) < t_col
        carry = tile(ks, carry, strict)
    n_full = q0 // tk

    def step(r, carry):
        ks = pl.multiple_of((n_full - 1 - r) * tk, tk)
        return tile(ks, carry, None)

    _, acc = lax.fori_loop(0, n_full, step, carry)
    o_ref[...] = acc.astype(o_ref.dtype)


def sb_mixer(proj, *, d_model, tq=256, tk=256):
    b, t, _ = proj.shape
    dh = HEAD_DIM
    n_heads = d_model // dh
    return pl.pallas_call(
        functools.partial(_sb_body, tk), grid=(b, n_heads, t // tq),
        in_specs=[pl.BlockSpec((None, tq, dh), lambda bi, h, i: (bi, i, h)),
                  pl.BlockSpec((None, t, dh), lambda bi, h, i: (bi, 0, n_heads + h)),
                  pl.BlockSpec((None, t, dh), lambda bi, h, i: (bi, 0, 2 * n_heads + h))],
        out_specs=pl.BlockSpec((None, tq, dh), lambda bi, h, i: (bi, i, h)),
        out_shape=jax.ShapeDtypeStruct((b, t, d_model), BF16),
        compiler_params=_cparams("parallel", "parallel", "arbitrary"), name="sb_mixer",
    )(proj, proj, proj)


def _gelu(x):
    return 0.5 * x * (1.0 + lax.erf(x * (2.0 ** -0.5)))


def _gmlp_body(p_ref, lg_ref, lb_ref, ws_ref, bs_ref, o_ref):
    tt, d = o_ref.shape
    gc = ws_ref.shape[1]
    v = _gelu(p_ref[:, d:2 * d].astype(F32))
    vc = v - jnp.mean(v, axis=-1, keepdims=True)
    vn = (vc * lax.rsqrt(jnp.mean(vc * vc, axis=-1, keepdims=True) + NORM_EPS) * lg_ref[...] + lb_ref[...]).astype(BF16)
    causal = lax.broadcasted_iota(jnp.int32, (gc, gc), 0) >= lax.broadcasted_iota(jnp.int32, (gc, gc), 1)
    for g in range(d // gc):
        sl = slice(g * gc, (g + 1) * gc)
        ws = jnp.where(causal, ws_ref[g], 0.0).astype(BF16)
        mixed = _dot(ws, vn[:, sl]) + bs_ref[:, sl]
        o_ref[:, sl] = (_gelu(p_ref[:, sl].astype(F32)) * mixed).astype(o_ref.dtype)


def gmlp_mixer(proj, ln_g, ln_b, w_s, b_s, *, d_model):
    b, t, _ = proj.shape
    d = d_model
    n_groups, gc, _ = w_s.shape
    row = lambda v: v.reshape(1, -1)
    const = lambda shape: pl.BlockSpec(shape, lambda bi, i: (0,) * len(shape))
    bs_cols = jnp.repeat(b_s.T, d // n_groups, axis=1)
    return pl.pallas_call(
        _gmlp_body, grid=(b, t // GM_CHUNK),
        in_specs=[pl.BlockSpec((None, GM_CHUNK, 2 * d), lambda bi, i: (bi, i, 0)),
                  const((1, d)), const((1, d)), const((n_groups, gc, gc)), const((GM_CHUNK, d))],
        out_specs=pl.BlockSpec((None, GM_CHUNK, d), lambda bi, i: (bi, i, 0)),
        out_shape=jax.ShapeDtypeStruct((b, t, d), BF16),
        compiler_params=_cparams("parallel", "parallel"), name="gmlp_mixer",
    )(proj, row(ln_g), row(ln_b), w_s, bs_cols)


def _rope_tables(positions):
    half = ROPE_DIM // 2
    inv = 1.0 / (ROPE_THETA ** (jnp.arange(0, ROPE_DIM, 2, dtype=F32) / ROPE_DIM))
    ang = positions.astype(F32)[..., None] * inv
    cos, sin = jnp.cos(ang), jnp.sin(ang)
    zeros = jnp.zeros_like(cos)
    tail = jnp.zeros(cos.shape[:-1] + (HEAD_DIM - ROPE_DIM,), F32)
    c = jnp.concatenate([cos, cos, tail + 1.0], axis=-1)
    s1 = jnp.concatenate([-sin, zeros, tail], axis=-1)
    s2 = jnp.concatenate([zeros, sin, tail], axis=-1)
    return c, s1, s2


def _pad_to(w, axis, mult):
    pad = (-w.shape[axis]) % mult
    if pad == 0:
        return w
    widths = [(0, 0)] * w.ndim
    widths[axis] = (0, pad)
    return jnp.pad(w, widths)


def _ffn(x2, norm_g, w_gu, w_down, d_ff, tn=512):
    wg = _pad_to(w_gu[:, :d_ff].astype(BF16), 1, tn)
    wu = _pad_to(w_gu[:, d_ff:].astype(BF16), 1, tn)
    wd = _pad_to(w_down.astype(BF16), 0, tn)
    act = norm_swiglu(x2, norm_g, wg, wu, tn=tn)
    return matmul_residual(act, wd, x2, 0.5)


def kernel(x, mem, positions, ffn1_norm, ffn1_w_gu, ffn1_w_down, mix_norm, ffn2_norm, ffn2_w_gu, ffn2_w_down, mem_norm, mem_w_kv, mem_k_norm, mem_q_norm, nsa_w_in, nsa_q_norm, nsa_k_norm, nsa_cmp_pos, nsa_cmp_w1, nsa_cmp_w2, nsa_w_out, conv_w_in, conv_b_in, conv_dw_w, conv_dw_b, conv_ln_g, conv_ln_b, conv_w_out, sb_w_in, sb_w_out, gm_w_in, gm_ln_g, gm_ln_b, gm_ws, gm_bs, gm_w_out):
    b, t, d = x.shape
    depth = ffn1_norm.shape[0]
    d_ff = ffn1_w_down.shape[1]
    mem_len = mem.shape[1]
    mem_width = mem_w_kv.shape[1] // 2
    n_heads = d // HEAD_DIM
    n_mixers = 4

    rope_tabs = _rope_tables(positions)
    mkv = norm_matmul(mem.reshape(b * mem_len, d), mem_norm, mem_w_kv.astype(BF16)).reshape(b, mem_len, 2 * mem_width)

    x2 = x.reshape(b * t, d)
    for i in range(depth):
        kind, j = i % n_mixers, i // n_mixers
        x2 = _ffn(x2, ffn1_norm[i], ffn1_w_gu[i], ffn1_w_down[i], d_ff)
        if kind == 0:
            w_in = nsa_w_in[j]
            n_main = w_in.shape[1] - mem_width - 3 * n_heads
            hpg = n_heads // NSA_KV_GROUPS
            w_gate = w_in[:, n_main:n_main + 3 * n_heads].reshape(d, NSA_KV_GROUPS, hpg, 3).transpose(0, 1, 3, 2)
            w_gate = _pad_to(w_gate.reshape(d, NSA_KV_GROUPS, 3 * hpg).astype(BF16), 2, LANES).reshape(d, -1)
            w_main = jnp.concatenate([w_in[:, :n_main], w_in[:, -mem_width:]], axis=1).astype(BF16)
            proj, gates = norm_matmul(x2, mix_norm[i], w_main, side_w=w_gate)
            proj = proj.reshape(b, t, -1)
            mixed = nsa_mixer(proj, gates, rope_tabs, nsa_q_norm[j], nsa_k_norm[j], nsa_cmp_pos[j],
                              nsa_cmp_w1[j], nsa_cmp_w2[j], d_model=d)
            w_out = nsa_w_out[j]
        elif kind == 1:
            proj = norm_matmul(x2, mix_norm[i], conv_w_in[j].astype(BF16)).reshape(b, t, -1)
            mixed = conv_mixer(proj, conv_b_in[j], conv_dw_w[j], conv_dw_b[j], conv_ln_g[j], conv_ln_b[j], d_model=d)
            w_out = conv_w_out[j]
        elif kind == 2:
            proj = norm_matmul(x2, mix_norm[i], sb_w_in[j].astype(BF16)).reshape(b, t, -1)
            mixed = sb_mixer(proj, d_model=d)
            w_out = sb_w_out[j]
        else:
            proj = norm_matmul(x2, mix_norm[i], gm_w_in[j].astype(BF16)).reshape(b, t, -1)
            mixed = gmlp_mixer(proj, gm_ln_g[j], gm_ln_b[j], gm_ws[j], gm_bs[j], d_model=d)
            w_out = gm_w_out[j]
        mem_out = mem_attention(proj, mkv, mem_q_norm[i], mem_k_norm, width=mem_width)
        w_out = w_out.astype(BF16)
        x2 = matmul2_residual(mixed.reshape(b * t, d), mem_out.reshape(b * t, mem_width), w_out[:d], w_out[d:], x2)
        x2 = _ffn(x2, ffn2_norm[i], ffn2_w_gu[i], ffn2_w_down[i], d_ff)
    return x2.reshape(b, t, d)
```

```python
import functools

import jax
import jax.numpy as jnp
from jax import lax
from jax.experimental import pallas as pl
from jax.experimental.pallas import tpu as pltpu

F32 = jnp.float32
BF16 = jnp.bfloat16

HEAD_DIM = 128
ROPE_THETA = 500000.0
ROPE_DIM = HEAD_DIM // 4
NORM_EPS = 1e-6
BIG = 1e30
NSA_KV_GROUPS = 4
CMP_BLOCK = 32
CMP_STRIDE = 16
SEL_BLOCK = 64
SEL_TOPK = 16
WINDOW = 512
CONV_WIDTH = 31
GM_CHUNK = 128
MEM_HEADS = 4

LANES = 128
MXU_WIDTH = 256
VMEM_LIMIT_BYTES = 56 * 1024 * 1024
SEL_MASK_BIAS = -32768.0
SB_SCORE_SCALE = HEAD_DIM ** -0.5 * 1.4426950408889634


def _cparams(*sem):
    return pltpu.CompilerParams(dimension_semantics=sem, vmem_limit_bytes=VMEM_LIMIT_BYTES)


def _dot(a, b):
    return jnp.dot(a, b, preferred_element_type=F32)


def _dot_nt(a, b):
    return lax.dot_general(a, b, (((1,), (1,)), ((), ())), preferred_element_type=F32)


def _floordiv_pow2(x, n):
    return lax.shift_right_logical(x, n.bit_length() - 1)


def _rms(x, g):
    return x * lax.rsqrt(jnp.mean(x * x, axis=-1, keepdims=True) + NORM_EPS) * g


def _norm_matmul_body(x_ref, g_ref, w_ref, o_ref, h_ref):
    @pl.when(pl.program_id(1) == 0)
    def _():
        h_ref[...] = _rms(x_ref[...], g_ref[...]).astype(BF16)

    o_ref[...] = _dot(h_ref[...], w_ref[...]).astype(o_ref.dtype)


def _norm_matmul_side_body(x_ref, g_ref, w_ref, ws_ref, o_ref, os_ref, h_ref):
    @pl.when(pl.program_id(1) == 0)
    def _():
        h = _rms(x_ref[...], g_ref[...]).astype(BF16)
        h_ref[...] = h
        os_ref[...] = jax.nn.sigmoid(_dot(h, ws_ref[...]))

    o_ref[...] = _dot(h_ref[...], w_ref[...]).astype(o_ref.dtype)


def norm_matmul(x, g, w, *, tm=512, tn=512, side_w=None):
    m, k = x.shape
    n = w.shape[1]
    tm = min(tm, m)
    grid = (m // tm, n // tn)
    x_spec = pl.BlockSpec((tm, k), lambda i, j: (i, 0))
    g_spec = pl.BlockSpec((1, k), lambda i, j: (0, 0))
    w_spec = pl.BlockSpec((k, tn), lambda i, j: (0, j))
    o_spec = pl.BlockSpec((tm, tn), lambda i, j: (i, j))
    scratch = [pltpu.VMEM((tm, k), BF16)]
    if side_w is None:
        return pl.pallas_call(
            _norm_matmul_body, grid=grid, in_specs=[x_spec, g_spec, w_spec], out_specs=o_spec,
            out_shape=jax.ShapeDtypeStruct((m, n), BF16), scratch_shapes=scratch,
            compiler_params=_cparams("parallel", "arbitrary"), name="norm_matmul")(x, g.reshape(1, k), w)
    ns = side_w.shape[1]
    return pl.pallas_call(
        _norm_matmul_side_body, grid=grid,
        in_specs=[x_spec, g_spec, w_spec, pl.BlockSpec((k, ns), lambda i, j: (0, 0))],
        out_specs=[o_spec, pl.BlockSpec((tm, ns), lambda i, j: (i, 0))],
        out_shape=[jax.ShapeDtypeStruct((m, n), BF16), jax.ShapeDtypeStruct((m, ns), F32)],
        scratch_shapes=scratch,
        compiler_params=_cparams("parallel", "arbitrary"), name="norm_matmul_side")(x, g.reshape(1, k), w, side_w)


def _norm_swiglu_body(x_ref, g_ref, wg_ref, wu_ref, o_ref, h_ref):
    @pl.when(pl.program_id(1) == 0)
    def _():
        h_ref[...] = _rms(x_ref[...], g_ref[...]).astype(BF16)

    h = h_ref[...]
    gate = _dot(h, wg_ref[...])
    up = _dot(h, wu_ref[...])
    o_ref[...] = (gate * jax.nn.sigmoid(gate) * up).astype(o_ref.dtype)


def norm_swiglu(x, g, wg, wu, *, tm=512, tn=512):
    m, k = x.shape
    n = wg.shape[1]
    tm = min(tm, m)
    grid = (m // tm, n // tn)
    return pl.pallas_call(
        _norm_swiglu_body, grid=grid,
        in_specs=[pl.BlockSpec((tm, k), lambda i, j: (i, 0)),
                  pl.BlockSpec((1, k), lambda i, j: (0, 0)),
                  pl.BlockSpec((k, tn), lambda i, j: (0, j)),
                  pl.BlockSpec((k, tn), lambda i, j: (0, j))],
        out_specs=pl.BlockSpec((tm, tn), lambda i, j: (i, j)),
        out_shape=jax.ShapeDtypeStruct((m, n), BF16),
        scratch_shapes=[pltpu.VMEM((tm, k), BF16)],
        compiler_params=_cparams("parallel", "arbitrary"), name="norm_swiglu")(x, g.reshape(1, k), wg, wu)


def _matmul_residual_body(scale, a_ref, w_ref, x_ref, o_ref):
    o_ref[...] = x_ref[...] + scale * _dot(a_ref[...], w_ref[...])


def matmul_residual(a, w, x, scale, *, tm=1024, tn=512):
    m, k = a.shape
    n = w.shape[1]
    tm = min(tm, m)
    return pl.pallas_call(
        functools.partial(_matmul_residual_body, scale), grid=(m // tm, n // tn),
        in_specs=[pl.BlockSpec((tm, k), lambda i, j: (i, 0)),
                  pl.BlockSpec((k, tn), lambda i, j: (0, j)),
                  pl.BlockSpec((tm, tn), lambda i, j: (i, j))],
        out_specs=pl.BlockSpec((tm, tn), lambda i, j: (i, j)),
        out_shape=jax.ShapeDtypeStruct((m, n), F32),
        compiler_params=_cparams("parallel", "arbitrary"), name="matmul_residual")(a, w, x)


def _matmul2_residual_body(a_ref, b_ref, wa_ref, wb_ref, x_ref, o_ref):
    o_ref[...] = x_ref[...] + _dot(a_ref[...], wa_ref[...]) + _dot(b_ref[...], wb_ref[...])


def matmul2_residual(a, b, wa, wb, x, *, tm=1024, tn=512):
    m, ka = a.shape
    kb = b.shape[1]
    n = wa.shape[1]
    tm = min(tm, m)
    return pl.pallas_call(
        _matmul2_residual_body, grid=(m // tm, n // tn),
        in_specs=[pl.BlockSpec((tm, ka), lambda i, j: (i, 0)),
                  pl.BlockSpec((tm, kb), lambda i, j: (i, 0)),
                  pl.BlockSpec((ka, tn), lambda i, j: (0, j)),
                  pl.BlockSpec((kb, tn), lambda i, j: (0, j)),
                  pl.BlockSpec((tm, tn), lambda i, j: (i, j))],
        out_specs=pl.BlockSpec((tm, tn), lambda i, j: (i, j)),
        out_shape=jax.ShapeDtypeStruct((m, n), F32),
        compiler_params=_cparams("parallel", "arbitrary"), name="matmul2_residual")(a, b, wa, wb, x)


def _mem_attn_body(heads, mq_ref, mkv_ref, qn_ref, kn_ref, o_ref):
    width = mq_ref.shape[-1]
    dh = width // heads
    scale = dh ** -0.5
    for h in range(heads):
        q = _rms(mq_ref[:, h * dh:(h + 1) * dh].astype(F32), qn_ref[...]) * scale
        k = _rms(mkv_ref[:, h * dh:(h + 1) * dh].astype(F32), kn_ref[...])
        v = mkv_ref[:, width + h * dh:width + (h + 1) * dh]
        s = _dot_nt(q.astype(BF16), k.astype(BF16))
        p = jnp.exp(s - jnp.max(s, axis=-1, keepdims=True))
        p = p / jnp.sum(p, axis=-1, keepdims=True)
        o_ref[:, h * dh:(h + 1) * dh] = _dot(p.astype(BF16), v).astype(o_ref.dtype)


def mem_attention(proj, mkv, q_norm, k_norm, *, width, tm=512):
    b, t, c = proj.shape
    mlen = mkv.shape[1]
    dh = width // MEM_HEADS
    qblk = (c - width) // width
    return pl.pallas_call(
        functools.partial(_mem_attn_body, MEM_HEADS), grid=(b, t // tm),
        in_specs=[pl.BlockSpec((None, tm, width), lambda bi, i: (bi, i, qblk)),
                  pl.BlockSpec((None, mlen, 2 * width), lambda bi, i: (bi, 0, 0)),
                  pl.BlockSpec((1, dh), lambda bi, i: (0, 0)),
                  pl.BlockSpec((1, dh), lambda bi, i: (0, 0))],
        out_specs=pl.BlockSpec((None, tm, width), lambda bi, i: (bi, i, 0)),
        out_shape=jax.ShapeDtypeStruct((b, t, width), BF16),
        compiler_params=_cparams("parallel", "parallel"), name="mem_attention",
    )(proj, mkv, q_norm.reshape(1, dh), k_norm.reshape(1, dh))


def _rope(y, c, s1, s2):
    half = ROPE_DIM // 2
    return y * c + pltpu.roll(y, HEAD_DIM - half, 1) * s1 + pltpu.roll(y, half, 1) * s2


def _nsa_prep_body(groups, q_ref, cmp_ref, sel_ref, win_ref, c_ref, s1_ref, s2_ref, qn_ref, kn_ref,
                   qo_ref, kcv_ref, ksa_ref, vsa_ref, kw_ref, vwa_ref):
    dh = HEAD_DIM
    tt = q_ref.shape[0]
    n_heads = q_ref.shape[1] // dh
    c, s1, s2 = c_ref[...], s1_ref[...], s2_ref[...]
    scale = dh ** -0.5
    qn = qn_ref[...] * scale
    for h in range(n_heads):
        y = _rms(q_ref[:, h * dh:(h + 1) * dh].astype(F32), qn)
        qo_ref[h] = _rope(y, c, s1, s2).astype(BF16)
    t = pl.program_id(1) * tt + lax.broadcasted_iota(jnp.int32, (tt, dh), 0)
    lane = lax.broadcasted_iota(jnp.int32, (tt, dh), 1)
    onehot = jnp.where(lane == _floordiv_pow2(t, SEL_BLOCK), 1.0, 0.0).astype(BF16)
    ones = jnp.ones((tt, dh), BF16)
    for g in range(groups):
        sl = slice(g * dh, (g + 1) * dh)
        vl = slice((groups + g) * dh, (groups + g + 1) * dh)
        lo = slice(2 * g * dh, (2 * g + 1) * dh)
        hi = slice((2 * g + 1) * dh, (2 * g + 2) * dh)
        kc = _rope(_rms(cmp_ref[:, sl].astype(F32), kn_ref[0:1, :]), c, s1, s2)
        kcv_ref[g] = kc.astype(BF16)
        kcv_ref[groups + g] = cmp_ref[:, vl]
        ks = _rope(_rms(sel_ref[:, sl].astype(F32), kn_ref[1:2, :]), c, s1, s2)
        ksa_ref[:, lo] = ks.astype(BF16)
        ksa_ref[:, hi] = onehot
        vsa_ref[:, lo] = sel_ref[:, vl]
        vsa_ref[:, hi] = ones
        kw = _rope(_rms(win_ref[:, sl].astype(F32), kn_ref[2:3, :]), c, s1, s2)
        kw_ref[:, sl] = kw.astype(BF16)
        vwa_ref[:, lo] = win_ref[:, vl]
        vwa_ref[:, hi] = ones


def nsa_prep(proj, rope_c, rope_s1, rope_s2, q_norm, k_norm, *, d_model, tt=256):
    b, t, _ = proj.shape
    g, dh = NSA_KV_GROUPS, HEAD_DIM
    kv2 = 2 * g * dh
    base = d_model // kv2
    tab = pl.BlockSpec((None, tt, dh), lambda bi, i: (bi, i, 0))
    return pl.pallas_call(
        functools.partial(_nsa_prep_body, g), grid=(b, t // tt),
        in_specs=[pl.BlockSpec((None, tt, d_model), lambda bi, i: (bi, i, 0)),
                  pl.BlockSpec((None, tt, kv2), lambda bi, i: (bi, i, base)),
                  pl.BlockSpec((None, tt, kv2), lambda bi, i: (bi, i, base + 1)),
                  pl.BlockSpec((None, tt, kv2), lambda bi, i: (bi, i, base + 2)),
                  tab, tab, tab,
                  pl.BlockSpec((1, dh), lambda bi, i: (0, 0)),
                  pl.BlockSpec((3, dh), lambda bi, i: (0, 0))],
        out_specs=[pl.BlockSpec((None, d_model // dh, tt, dh), lambda bi, i: (bi, 0, i, 0)),
                   pl.BlockSpec((None, 2 * g, tt, dh), lambda bi, i: (bi, 0, i, 0)),
                   pl.BlockSpec((None, tt, 2 * g * dh), lambda bi, i: (bi, i, 0)),
                   pl.BlockSpec((None, tt, 2 * g * dh), lambda bi, i: (bi, i, 0)),
                   pl.BlockSpec((None, tt, g * dh), lambda bi, i: (bi, i, 0)),
                   pl.BlockSpec((None, tt, 2 * g * dh), lambda bi, i: (bi, i, 0))],
        out_shape=[jax.ShapeDtypeStruct((b, d_model // dh, t, dh), BF16),
                   jax.ShapeDtypeStruct((b, 2 * g, t, dh), BF16),
                   jax.ShapeDtypeStruct((b, t, 2 * g * dh), BF16),
                   jax.ShapeDtypeStruct((b, t, 2 * g * dh), BF16),
                   jax.ShapeDtypeStruct((b, t, g * dh), BF16),
                   jax.ShapeDtypeStruct((b, t, 2 * g * dh), BF16)],
        compiler_params=_cparams("parallel", "parallel"), name="nsa_prep",
    )(proj, proj, proj, proj, rope_c, rope_s1, rope_s2, q_norm.reshape(1, dh), k_norm)


def _nsa_compress_body(p_ref, w1_ref, w2_ref, pos_ref, o_ref):
    n_pieces, half = p_ref.shape
    p = p_ref[...]
    u = _dot(p, w1_ref[0:half, :])
    v = _dot(p, w1_ref[half:2 * half, :])
    pos = jnp.broadcast_to(pos_ref[...], (8, 2 * half)).astype(BF16)
    c = _dot(pos, w1_ref[...])[0:1, :]
    hid = u + pltpu.roll(v, n_pieces - 1, 0) + c
    hid = hid * jax.nn.sigmoid(hid)
    o_ref[...] = _dot(hid.astype(BF16), w2_ref[...]).astype(o_ref.dtype)


def nsa_compress(kcv, w1, w2, pos):
    b, g2, n_pieces, half = kcv.shape
    dh = HEAD_DIM
    g = g2 // 2
    return pl.pallas_call(
        _nsa_compress_body, grid=(b, g2),
        in_specs=[pl.BlockSpec((None, None, n_pieces, half), lambda bi, gi: (bi, gi, 0, 0)),
                  pl.BlockSpec((None, 2 * half, dh), lambda bi, gi: (gi // g, 0, 0)),
                  pl.BlockSpec((None, dh, dh), lambda bi, gi: (gi // g, 0, 0)),
                  pl.BlockSpec((None, 1, 2 * half), lambda bi, gi: (gi // g, 0, 0))],
        out_specs=pl.BlockSpec((None, None, n_pieces, dh), lambda bi, gi: (bi, gi, 0, 0)),
        out_shape=jax.ShapeDtypeStruct((b, g2, n_pieces, dh), BF16),
        compiler_params=_cparams("parallel", "parallel"), name="nsa_compress",
    )(kcv, w1, w2, pos)


def _nsa_attn_body(tk, q_ref, gate_ref, kc_ref, vc_ref, ksa_ref, vsa_ref, kw_ref, vwa_ref, o_ref, mx_ref, acc_ref, s_ref):
    dh = HEAD_DIM
    hpg, tq, _ = q_ref.shape
    rows = hpg * tq
    n_cmp = kc_ref.shape[0]
    q0 = pl.program_id(2) * tq
    q_all = q_ref[...].reshape(rows, dh)
    t_one = q0 + lax.broadcasted_iota(jnp.int32, (tq, 1), 0)
    t_col = q0 + (lax.broadcasted_iota(jnp.int32, (rows, 1), 0) & (tq - 1))

    n_row = lax.broadcasted_iota(jnp.int32, (1, n_cmp), 1)
    cbias = jnp.where((n_row * CMP_STRIDE + (CMP_BLOCK - 1)) <= t_one, 0.0, -BIG)
    s_cmp = (_dot_nt(q_all, kc_ref[...]).reshape(hpg, tq, n_cmp) + cbias[None]).reshape(rows, n_cmp)
    p_cmp = jnp.exp(s_cmp - jnp.max(s_cmp, axis=-1, keepdims=True))
    inv = jnp.where(t_col >= CMP_BLOCK - 1, 1.0 / jnp.sum(p_cmp, axis=-1, keepdims=True), 0.0)
    p_cmp = p_cmp * inv
    o_cmp = _dot(p_cmp.astype(BF16), vc_ref[...])
    imp = jnp.sum(p_cmp.reshape(hpg, tq, n_cmp), axis=0)

    rs = SEL_BLOCK // CMP_STRIDE
    n_sel = ksa_ref.shape[0] // SEL_BLOCK
    mm = lax.broadcasted_iota(jnp.int32, (n_sel, n_cmp), 0)
    nn = lax.broadcasted_iota(jnp.int32, (n_sel, n_cmp), 1)
    pool_t = jnp.where((nn >= rs * mm - (CMP_BLOCK // CMP_STRIDE - 1)) & (nn <= rs * mm + rs - 1), 1.0, 0.0).astype(BF16)
    imp_hi = imp.astype(BF16)
    imp_lo = (imp - imp_hi.astype(F32)).astype(BF16)
    imp_sel = _dot_nt(pool_t, imp_hi) + _dot_nt(pool_t, imp_lo)

    blk = lax.broadcasted_iota(jnp.int32, (n_sel, tq), 0)
    cur = _floordiv_pow2(q0 + lax.broadcasted_iota(jnp.int32, (n_sel, tq), 1), SEL_BLOCK)
    forced = (blk == 0) | (blk == cur) | (blk == cur - 1)
    valid = blk <= cur
    score = jnp.where(valid, jnp.where(forced, BIG, imp_sel), -BIG)
    sub = 8
    groups = [score[g * sub:(g + 1) * sub, :] for g in range(n_sel // sub)]
    ranks = [jnp.zeros((sub, tq), F32) for _ in groups]
    sub_iota = lax.broadcasted_iota(jnp.int32, (sub, tq), 0)
    for m in range(n_sel):
        row = jnp.broadcast_to(score[m:m + 1, :], (sub, tq))
        for g, sg in enumerate(groups):
            if g * sub > m:
                inc = jnp.where(row >= sg, 1.0, 0.0)
            elif g * sub + sub - 1 <= m:
                inc = jnp.where(row > sg, 1.0, 0.0)
            else:
                inc = jnp.where(sub_iota + g * sub > m, jnp.where(row >= sg, 1.0, 0.0), jnp.where(row > sg, 1.0, 0.0))
            ranks[g] = ranks[g] + inc
    rank = jnp.concatenate(ranks, axis=0)
    selected = valid & (rank < float(min(SEL_TOPK, n_sel)))
    bias_t = jnp.where(selected, 0.0, SEL_MASK_BIAS)
    if n_sel < LANES:
        bias_t = jnp.concatenate([bias_t, jnp.zeros((LANES - n_sel, tq), F32)], axis=0)
    sel_bias = bias_t.T.astype(BF16)

    n_split = 2
    hr = rows // n_split
    q_parts = [q_all[i * hr:(i + 1) * hr] for i in range(n_split)]
    n_full = q0 // tk
    d_start = pl.multiple_of(n_full * tk, tk)

    def scores(ks):
        k = ksa_ref[pl.ds(ks, tk), :]
        bias = _dot_nt(sel_bias, k[:, dh:2 * dh])
        return [(_dot_nt(qp, k[:, 0:dh]).reshape(hpg // n_split, tq, tk) + bias[None]).reshape(hr, tk) for qp in q_parts]

    def fold(s):
        out = s[:, 0:LANES]
        for c in range(1, tk // LANES):
            out = jnp.maximum(out, s[:, c * LANES:(c + 1) * LANES])
        return out

    dmask = (d_start + lax.broadcasted_iota(jnp.int32, (1, tk), 1)) <= t_col[0:hr]
    for i, s in enumerate(scores(d_start)):
        r = slice(i * hr, (i + 1) * hr)
        s = jnp.where(dmask, s, -BIG)
        s_ref[n_full, r, :] = s
        mx_ref[r, :] = fold(s)

    def max_tiles(kts):
        all_s = [scores(pl.multiple_of(kt * tk, tk)) for kt in kts]
        for kt, parts in zip(kts, all_s):
            for i, s in enumerate(parts):
                r = slice(i * hr, (i + 1) * hr)
                s_ref[kt, r, :] = s
                mx_ref[r, :] = jnp.maximum(mx_ref[r, :], fold(s))

    def max_step(r, carry):
        max_tiles([2 * r, 2 * r + 1])
        return carry

    lax.fori_loop(0, n_full // 2, max_step, 0)

    @pl.when((n_full & 1) == 1)
    def _():
        max_tiles([n_full - 1])

    m_row = jnp.broadcast_to(jnp.max(mx_ref[...], axis=-1, keepdims=True), (rows, LANES))
    m_row = jnp.concatenate([m_row] * (tk // LANES), axis=-1)
    m_parts = [m_row[i * hr:(i + 1) * hr] for i in range(n_split)]

    acc_ref[...] = jnp.zeros_like(acc_ref)

    def acc_tiles(kts):
        ps = [[jnp.exp(s_ref[kt, i * hr:(i + 1) * hr, :] - m_parts[i]).astype(BF16) for i in range(n_split)] for kt in kts]
        for kt, parts in zip(kts, ps):
            v = vsa_ref[pl.ds(pl.multiple_of(kt * tk, tk), tk), :]
            for i, p in enumerate(parts):
                r = slice(i * hr, (i + 1) * hr)
                acc_ref[r, :] += _dot(p, v)

    def acc_step(r, carry):
        acc_tiles([2 * r, 2 * r + 1])
        return carry

    n_tiles = n_full + 1
    lax.fori_loop(0, n_tiles // 2, acc_step, 0)

    @pl.when((n_tiles & 1) == 1)
    def _():
        acc_tiles([n_tiles - 1])

    acc = acc_ref[...]
    o_sel = acc[:, 0:dh] / acc[:, dh:2 * dh]

    w_len = WINDOW + tq
    w_start = pl.multiple_of(jnp.maximum(q0 - WINDOW, 0), tq)
    diff = t_one - (w_start + lax.broadcasted_iota(jnp.int32, (1, w_len), 1))
    wbias = jnp.where((diff >= 0) & (diff < WINDOW), 0.0, -BIG)
    s_win = (_dot_nt(q_all, kw_ref[pl.ds(w_start, w_len), :]).reshape(hpg, tq, w_len) + wbias[None]).reshape(rows, w_len)
    p_win = jnp.exp(s_win - jnp.max(s_win, axis=-1, keepdims=True))
    acc_w = _dot(p_win.astype(BF16), vwa_ref[pl.ds(w_start, w_len), :])
    o_win = acc_w[:, 0:dh] / acc_w[:, dh:2 * dh]

    for j in range(hpg):
        r = slice(j * tq, (j + 1) * tq)
        g_cmp = gate_ref[:, j:j + 1]
        g_sel = gate_ref[:, hpg + j:hpg + j + 1]
        g_win = gate_ref[:, 2 * hpg + j:2 * hpg + j + 1]
        o_ref[:, j * dh:(j + 1) * dh] = (g_cmp * o_cmp[r] + g_sel * o_sel[r] + g_win * o_win[r]).astype(o_ref.dtype)


def nsa_attention(qh, gates, kv_cmp, ks_aug, vs_aug, kw, vw_aug, *, d_model, tq=128, tk=256):
    b, _, t, dh = qh.shape
    g = NSA_KV_GROUPS
    hpg = d_model // dh // g
    n_cmp = kv_cmp.shape[2]
    aug = pl.BlockSpec((None, t, 2 * dh), lambda bi, gi, i: (bi, 0, gi))
    return pl.pallas_call(
        functools.partial(_nsa_attn_body, tk), grid=(b, g, t // tq),
        in_specs=[pl.BlockSpec((None, hpg, tq, dh), lambda bi, gi, i: (bi, gi, i, 0)),
                  pl.BlockSpec((None, tq, LANES), lambda bi, gi, i: (bi, i, gi)),
                  pl.BlockSpec((None, None, n_cmp, dh), lambda bi, gi, i: (bi, gi, 0, 0)),
                  pl.BlockSpec((None, None, n_cmp, dh), lambda bi, gi, i: (bi, g + gi, 0, 0)),
                  aug, aug,
                  pl.BlockSpec((None, t, dh), lambda bi, gi, i: (bi, 0, gi)),
                  aug],
        out_specs=pl.BlockSpec((None, tq, hpg * dh), lambda bi, gi, i: (bi, i, gi)),
        out_shape=jax.ShapeDtypeStruct((b, t, d_model), BF16),
        scratch_shapes=[pltpu.VMEM((hpg * tq, LANES), F32), pltpu.VMEM((hpg * tq, 2 * dh), F32),
                        pltpu.VMEM((t // tk, hpg * tq, tk), F32)],
        compiler_params=_cparams("parallel", "parallel", "arbitrary"), name="nsa_attention",
    )(qh, gates, kv_cmp, kv_cmp, ks_aug, vs_aug, kw, vw_aug)


def nsa_mixer(proj, gates, rope_tabs, q_norm, k_norm, cmp_pos, cmp_w1, cmp_w2, *, d_model):
    b, t, _ = proj.shape
    dh = HEAD_DIM
    qh, kcv, ks_aug, vs_aug, kw, vw_aug = nsa_prep(proj, *rope_tabs, q_norm, k_norm, d_model=d_model)
    n_pieces = t // CMP_STRIDE
    kcv = kcv.reshape(b, 2 * NSA_KV_GROUPS, n_pieces, CMP_STRIDE * dh)
    kv_cmp = nsa_compress(kcv, cmp_w1.astype(BF16), cmp_w2.astype(BF16), cmp_pos.reshape(2, 1, CMP_BLOCK * dh))
    return nsa_attention(qh, gates.reshape(b, t, NSA_KV_GROUPS * LANES), kv_cmp, ks_aug, vs_aug, kw, vw_aug, d_model=d_model)


def _conv_body(halo, cw, p_ref, ph_ref, bin_ref, w_ref, wb_ref, lg_ref, lb_ref, o_ref, u_ref, y_ref):
    tt, d = o_ref.shape
    i = pl.program_id(1)

    def glu(ref):
        a = ref[:, 0:d].astype(F32) + bin_ref[:, 0:d]
        gate = ref[:, d:2 * d].astype(F32) + bin_ref[:, d:2 * d]
        return a * jax.nn.sigmoid(gate)

    u_ref[halo:halo + tt, :] = glu(p_ref)

    @pl.when(i == 0)
    def _():
        u_ref[0:halo, :] = jnp.zeros((halo, d), F32)

    @pl.when(i > 0)
    def _():
        u_ref[0:halo, :] = glu(ph_ref)

    off = halo - (CONV_WIDTH - 1)
    for c0 in range(0, d, cw):
        acc = jnp.zeros((tt, cw), F32)
        for k in range(CONV_WIDTH):
            acc = acc + w_ref[k:k + 1, c0:c0 + cw] * u_ref[off + k:off + k + tt, c0:c0 + cw]
        y_ref[:, c0:c0 + cw] = acc + wb_ref[:, c0:c0 + cw]

    y = y_ref[...]
    yc = y - jnp.mean(y, axis=-1, keepdims=True)
    yn = yc * lax.rsqrt(jnp.mean(yc * yc, axis=-1, keepdims=True) + NORM_EPS) * lg_ref[...] + lb_ref[...]
    o_ref[...] = (yn * jax.nn.sigmoid(yn)).astype(o_ref.dtype)


def conv_mixer(proj, b_in, dw_w, dw_b, ln_g, ln_b, *, d_model, tt=256, halo=32, cw=512):
    b, t, _ = proj.shape
    d = d_model
    hb = tt // halo
    row = lambda v: v.reshape(1, -1)
    const = lambda shape: pl.BlockSpec(shape, lambda bi, i: (0, 0))
    return pl.pallas_call(
        functools.partial(_conv_body, halo, cw), grid=(b, t // tt),
        in_specs=[pl.BlockSpec((None, tt, 2 * d), lambda bi, i: (bi, i, 0)),
                  pl.BlockSpec((None, halo, 2 * d), lambda bi, i: (bi, jnp.maximum(i * hb - 1, 0), 0)),
                  const((1, 2 * d)), const((CONV_WIDTH, d)), const((1, d)), const((1, d)), const((1, d))],
        out_specs=pl.BlockSpec((None, tt, d), lambda bi, i: (bi, i, 0)),
        out_shape=jax.ShapeDtypeStruct((b, t, d), BF16),
        scratch_shapes=[pltpu.VMEM((halo + tt, d), F32), pltpu.VMEM((tt, d), F32)],
        compiler_params=_cparams("parallel", "arbitrary"), name="conv_mixer",
    )(proj, proj, row(b_in), dw_w, row(dw_b), row(ln_g), row(ln_b))


def _sb_body(tk, q_ref, k_ref, v_ref, o_ref):
    tq = q_ref.shape[0]
    dh = HEAD_DIM
    hp = q_ref.shape[1] // dh
    heads = [slice(h * dh, (h + 1) * dh) for h in range(hp)]
    q0 = pl.program_id(2) * tq
    ss = lax.broadcasted_iota(jnp.int32, (tk, tk), 0)
    jj = lax.broadcasted_iota(jnp.int32, (tk, tk), 1)
    suffix = jnp.where(ss > jj, 1.0, 0.0).astype(BF16)

    def tiles(ks, carries, strict):
        zs = [_dot_nt(q_ref[:, hs], k_ref[pl.ds(ks, tk), hs]) for hs in heads]
        log_betas, log_keeps = [], []
        for z in zs:
            neg_abs = lax.bitcast_convert_type(lax.bitcast_convert_type(z, jnp.uint32) | jnp.uint32(0x80000000), F32)
            log_beta = jnp.minimum(z, 0.0) - jnp.log2(1.0 + jnp.exp2(neg_abs))
            log_keep = log_beta - z
            if strict is not None:
                log_keep = jnp.where(strict, log_keep, 0.0)
            log_betas.append(log_beta)
            log_keeps.append(log_keep)
        afters = [_dot(lk.astype(BF16), suffix) for lk in log_keeps]
        out = []
        for h in range(hp):
            c_run, acc = carries[h]
            a = jnp.exp2(log_betas[h] + afters[h] + c_run)
            if strict is not None:
                a = jnp.where(strict, a, 0.0)
            acc = acc + _dot(a.astype(BF16), v_ref[pl.ds(ks, tk), heads[h]])
            c_run = c_run + jnp.sum(log_keeps[h], axis=-1, keepdims=True)
            out.append((c_run, acc))
        return tuple(out)

    carries = tuple((jnp.zeros((tq, 1), F32), jnp.zeros((tq, dh), F32)) for _ in range(hp))
    t_col = q0 + lax.broadcasted_iota(jnp.int32, (tq, 1), 0)
    for d in range(tq // tk - 1, -1, -1):
        ks = pl.multiple_of(q0 + d * tk, tk)
        strict = (ks + lax.broadcasted_iota(jnp.int32, (1, tk), 1)) < t_col
        carries = tiles(ks, carries, strict)
    n_full = q0 // tk

    def step(r, carries):
        return tiles(pl.multiple_of((n_full - 1 - r) * tk, tk), carries, None)

    carries = lax.fori_loop(0, n_full, step, carries)
    for h in range(hp):
        o_ref[:, heads[h]] = carries[h][1].astype(o_ref.dtype)


def sb_mixer(proj, *, d_model, tq=256, tk=256, hp=4):
    b, t, _ = proj.shape
    w = hp * HEAD_DIM
    nb = d_model // w
    return pl.pallas_call(
        functools.partial(_sb_body, tk), grid=(b, nb, t // tq),
        in_specs=[pl.BlockSpec((None, tq, w), lambda bi, h, i: (bi, i, h)),
                  pl.BlockSpec((None, t, w), lambda bi, h, i: (bi, 0, nb + h)),
                  pl.BlockSpec((None, t, w), lambda bi, h, i: (bi, 0, 2 * nb + h))],
        out_specs=pl.BlockSpec((None, tq, w), lambda bi, h, i: (bi, i, h)),
        out_shape=jax.ShapeDtypeStruct((b, t, d_model), BF16),
        compiler_params=_cparams("parallel", "parallel", "arbitrary"), name="sb_mixer",
    )(proj, proj, proj)


def _gelu(x):
    return 0.5 * x * (1.0 + lax.erf(x * (2.0 ** -0.5)))


def _gmlp_body(p_ref, lg_ref, lb_ref, ws_ref, bs_ref, o_ref):
    tt, d = o_ref.shape
    gc = ws_ref.shape[1]
    v = _gelu(p_ref[:, d:2 * d].astype(F32))
    vc = v - jnp.mean(v, axis=-1, keepdims=True)
    vn = (vc * lax.rsqrt(jnp.mean(vc * vc, axis=-1, keepdims=True) + NORM_EPS) * lg_ref[...] + lb_ref[...]).astype(BF16)
    causal = lax.broadcasted_iota(jnp.int32, (gc, gc), 0) >= lax.broadcasted_iota(jnp.int32, (gc, gc), 1)
    for g in range(d // gc):
        sl = slice(g * gc, (g + 1) * gc)
        ws = jnp.where(causal, ws_ref[g], 0.0).astype(BF16)
        mixed = _dot(ws, vn[:, sl]) + bs_ref[:, sl]
        o_ref[:, sl] = (_gelu(p_ref[:, sl].astype(F32)) * mixed).astype(o_ref.dtype)


def gmlp_mixer(proj, ln_g, ln_b, w_s, b_s, *, d_model):
    b, t, _ = proj.shape
    d = d_model
    n_groups, gc, _ = w_s.shape
    row = lambda v: v.reshape(1, -1)
    const = lambda shape: pl.BlockSpec(shape, lambda bi, i: (0,) * len(shape))
    bs_cols = jnp.repeat(b_s.T, d // n_groups, axis=1)
    return pl.pallas_call(
        _gmlp_body, grid=(b, t // GM_CHUNK),
        in_specs=[pl.BlockSpec((None, GM_CHUNK, 2 * d), lambda bi, i: (bi, i, 0)),
                  const((1, d)), const((1, d)), const((n_groups, gc, gc)), const((GM_CHUNK, d))],
        out_specs=pl.BlockSpec((None, GM_CHUNK, d), lambda bi, i: (bi, i, 0)),
        out_shape=jax.ShapeDtypeStruct((b, t, d), BF16),
        compiler_params=_cparams("parallel", "parallel"), name="gmlp_mixer",
    )(proj, row(ln_g), row(ln_b), w_s, bs_cols)


def _rope_tables(positions):
    half = ROPE_DIM // 2
    inv = 1.0 / (ROPE_THETA ** (jnp.arange(0, ROPE_DIM, 2, dtype=F32) / ROPE_DIM))
    ang = positions.astype(F32)[..., None] * inv
    cos, sin = jnp.cos(ang), jnp.sin(ang)
    zeros = jnp.zeros_like(cos)
    tail = jnp.zeros(cos.shape[:-1] + (HEAD_DIM - ROPE_DIM,), F32)
    c = jnp.concatenate([cos, cos, tail + 1.0], axis=-1)
    s1 = jnp.concatenate([-sin, zeros, tail], axis=-1)
    s2 = jnp.concatenate([zeros, sin, tail], axis=-1)
    return c, s1, s2


def _pad_to(w, axis, mult):
    pad = (-w.shape[axis]) % mult
    if pad == 0:
        return w
    widths = [(0, 0)] * w.ndim
    widths[axis] = (0, pad)
    return jnp.pad(w, widths)


def _ffn(x2, norm_g, w_gu, w_down, d_ff, tn=512):
    wg = _pad_to(w_gu[:, :d_ff].astype(BF16), 1, tn)
    wu = _pad_to(w_gu[:, d_ff:].astype(BF16), 1, tn)
    wd = _pad_to(w_down.astype(BF16), 0, tn)
    act = norm_swiglu(x2, norm_g, wg, wu, tn=tn)
    return matmul_residual(act, wd, x2, 0.5)


def kernel(x, mem, positions, ffn1_norm, ffn1_w_gu, ffn1_w_down, mix_norm, ffn2_norm, ffn2_w_gu, ffn2_w_down, mem_norm, mem_w_kv, mem_k_norm, mem_q_norm, nsa_w_in, nsa_q_norm, nsa_k_norm, nsa_cmp_pos, nsa_cmp_w1, nsa_cmp_w2, nsa_w_out, conv_w_in, conv_b_in, conv_dw_w, conv_dw_b, conv_ln_g, conv_ln_b, conv_w_out, sb_w_in, sb_w_out, gm_w_in, gm_ln_g, gm_ln_b, gm_ws, gm_bs, gm_w_out):
    b, t, d = x.shape
    depth = ffn1_norm.shape[0]
    d_ff = ffn1_w_down.shape[1]
    mem_len = mem.shape[1]
    mem_width = mem_w_kv.shape[1] // 2
    n_heads = d // HEAD_DIM
    n_mixers = 4

    rope_tabs = _rope_tables(positions)
    mkv = norm_matmul(mem.reshape(b * mem_len, d), mem_norm, mem_w_kv.astype(BF16)).reshape(b, mem_len, 2 * mem_width)

    x2 = x.reshape(b * t, d)
    for i in range(depth):
        kind, j = i % n_mixers, i // n_mixers
        x2 = _ffn(x2, ffn1_norm[i], ffn1_w_gu[i], ffn1_w_down[i], d_ff)
        if kind == 0:
            w_in = nsa_w_in[j]
            n_main = w_in.shape[1] - mem_width - 3 * n_heads
            hpg = n_heads // NSA_KV_GROUPS
            w_gate = w_in[:, n_main:n_main + 3 * n_heads].reshape(d, NSA_KV_GROUPS, hpg, 3).transpose(0, 1, 3, 2)
            w_gate = _pad_to(w_gate.reshape(d, NSA_KV_GROUPS, 3 * hpg).astype(BF16), 2, LANES).reshape(d, -1)
            w_main = jnp.concatenate([w_in[:, :n_main], w_in[:, -mem_width:]], axis=1).astype(BF16)
            proj, gates = norm_matmul(x2, mix_norm[i], w_main, side_w=w_gate)
            proj = proj.reshape(b, t, -1)
            mixed = nsa_mixer(proj, gates, rope_tabs, nsa_q_norm[j], nsa_k_norm[j], nsa_cmp_pos[j],
                              nsa_cmp_w1[j], nsa_cmp_w2[j], d_model=d)
            w_out = nsa_w_out[j]
        elif kind == 1:
            proj = norm_matmul(x2, mix_norm[i], conv_w_in[j].astype(BF16)).reshape(b, t, -1)
            mixed = conv_mixer(proj, conv_b_in[j], conv_dw_w[j], conv_dw_b[j], conv_ln_g[j], conv_ln_b[j], d_model=d)
            w_out = conv_w_out[j]
        elif kind == 2:
            w_in = sb_w_in[j]
            w_in = jnp.concatenate([w_in[:, :d] * SB_SCORE_SCALE, w_in[:, d:]], axis=1).astype(BF16)
            proj = norm_matmul(x2, mix_norm[i], w_in).reshape(b, t, -1)
            mixed = sb_mixer(proj, d_model=d)
            w_out = sb_w_out[j]
        else:
            proj = norm_matmul(x2, mix_norm[i], gm_w_in[j].astype(BF16)).reshape(b, t, -1)
            mixed = gmlp_mixer(proj, gm_ln_g[j], gm_ln_b[j], gm_ws[j], gm_bs[j], d_model=d)
            w_out = gm_w_out[j]
        mem_out = mem_attention(proj, mkv, mem_q_norm[i], mem_k_norm, width=mem_width)
        w_out = w_out.astype(BF16)
        x2 = matmul2_residual(mixed.reshape(b * t, d), mem_out.reshape(b * t, mem_width), w_out[:d], w_out[d:], x2)
        x2 = _ffn(x2, ffn2_norm[i], ffn2_w_gu[i], ffn2_w_down[i], d_ff)
    return x2.reshape(b, t, d)
```

```python
import functools

import jax
import jax.numpy as jnp
from jax import lax
from jax.experimental import pallas as pl
from jax.experimental.pallas import tpu as pltpu

F32 = jnp.float32
BF16 = jnp.bfloat16

HEAD_DIM = 128
ROPE_THETA = 500000.0
ROPE_DIM = HEAD_DIM // 4
NORM_EPS = 1e-6
BIG = 1e30
NSA_KV_GROUPS = 4
CMP_BLOCK = 32
CMP_STRIDE = 16
SEL_BLOCK = 64
SEL_TOPK = 16
WINDOW = 512
CONV_WIDTH = 31
GM_CHUNK = 128
MEM_HEADS = 4

LANES = 128
SUBLANES = 8
VMEM_LIMIT_BYTES = 56 * 1024 * 1024
SEL_MASK_BIAS = -32768.0
SB_SCORE_SCALE = HEAD_DIM ** -0.5 * 1.4426950408889634


def _cparams(*sem):
    return pltpu.CompilerParams(dimension_semantics=sem, vmem_limit_bytes=VMEM_LIMIT_BYTES)


def _dot(a, b):
    return jnp.dot(a, b, preferred_element_type=F32)


def _dot_nt(a, b):
    return lax.dot_general(a, b, (((1,), (1,)), ((), ())), preferred_element_type=F32)


def _floordiv_pow2(x, n):
    return lax.shift_right_logical(x, n.bit_length() - 1)


def _rms(x, g):
    return x * lax.rsqrt(jnp.mean(x * x, axis=-1, keepdims=True) + NORM_EPS) * g


def _norm_matmul_body(x_ref, g_ref, w_ref, o_ref, h_ref):
    @pl.when(pl.program_id(1) == 0)
    def _():
        h_ref[...] = _rms(x_ref[...], g_ref[...]).astype(BF16)

    o_ref[...] = _dot(h_ref[...], w_ref[...]).astype(o_ref.dtype)


def _norm_matmul_side_body(x_ref, g_ref, w_ref, ws_ref, o_ref, os_ref, h_ref):
    @pl.when(pl.program_id(1) == 0)
    def _():
        h = _rms(x_ref[...], g_ref[...]).astype(BF16)
        h_ref[...] = h
        os_ref[...] = jax.nn.sigmoid(_dot(h, ws_ref[...]))

    o_ref[...] = _dot(h_ref[...], w_ref[...]).astype(o_ref.dtype)


def norm_matmul(x, g, w, *, tm=512, tn=512, side_w=None):
    m, k = x.shape
    n = w.shape[1]
    tm = min(tm, m)
    grid = (m // tm, n // tn)
    x_spec = pl.BlockSpec((tm, k), lambda i, j: (i, 0))
    g_spec = pl.BlockSpec((1, k), lambda i, j: (0, 0))
    w_spec = pl.BlockSpec((k, tn), lambda i, j: (0, j))
    o_spec = pl.BlockSpec((tm, tn), lambda i, j: (i, j))
    scratch = [pltpu.VMEM((tm, k), BF16)]
    if side_w is None:
        return pl.pallas_call(
            _norm_matmul_body, grid=grid, in_specs=[x_spec, g_spec, w_spec], out_specs=o_spec,
            out_shape=jax.ShapeDtypeStruct((m, n), BF16), scratch_shapes=scratch,
            compiler_params=_cparams("parallel", "arbitrary"), name="norm_matmul")(x, g.reshape(1, k), w)
    ns = side_w.shape[1]
    return pl.pallas_call(
        _norm_matmul_side_body, grid=grid,
        in_specs=[x_spec, g_spec, w_spec, pl.BlockSpec((k, ns), lambda i, j: (0, 0))],
        out_specs=[o_spec, pl.BlockSpec((tm, ns), lambda i, j: (i, 0))],
        out_shape=[jax.ShapeDtypeStruct((m, n), BF16), jax.ShapeDtypeStruct((m, ns), F32)],
        scratch_shapes=scratch,
        compiler_params=_cparams("parallel", "arbitrary"), name="norm_matmul_side")(x, g.reshape(1, k), w, side_w)


def _norm_swiglu_body(x_ref, g_ref, wg_ref, wu_ref, o_ref, h_ref):
    @pl.when(pl.program_id(1) == 0)
    def _():
        h_ref[...] = _rms(x_ref[...], g_ref[...]).astype(BF16)

    h = h_ref[...]
    gate = _dot(h, wg_ref[...])
    up = _dot(h, wu_ref[...])
    o_ref[...] = (gate * jax.nn.sigmoid(gate) * up).astype(o_ref.dtype)


def norm_swiglu(x, g, w_gu, *, tm=512, tn=512):
    m, k = x.shape
    n = w_gu.shape[1] // 2
    tm = min(tm, m)
    nt = n // tn
    return pl.pallas_call(
        _norm_swiglu_body, grid=(m // tm, nt),
        in_specs=[pl.BlockSpec((tm, k), lambda i, j: (i, 0)),
                  pl.BlockSpec((1, k), lambda i, j: (0, 0)),
                  pl.BlockSpec((k, tn), lambda i, j: (0, j)),
                  pl.BlockSpec((k, tn), lambda i, j: (0, nt + j))],
        out_specs=pl.BlockSpec((tm, tn), lambda i, j: (i, j)),
        out_shape=jax.ShapeDtypeStruct((m, n), BF16),
        scratch_shapes=[pltpu.VMEM((tm, k), BF16)],
        compiler_params=_cparams("parallel", "arbitrary"), name="norm_swiglu")(x, g.reshape(1, k), w_gu, w_gu)


def _matmul_residual_body(scale, a_ref, w_ref, x_ref, o_ref):
    o_ref[...] = x_ref[...] + scale * _dot(a_ref[...], w_ref[...])


def matmul_residual(a, w, x, scale, *, tm=1024, tn=512):
    m = a.shape[0]
    k, n = w.shape
    tm = min(tm, m)
    return pl.pallas_call(
        functools.partial(_matmul_residual_body, scale), grid=(m // tm, n // tn),
        in_specs=[pl.BlockSpec((tm, k), lambda i, j: (i, 0)),
                  pl.BlockSpec((k, tn), lambda i, j: (0, j)),
                  pl.BlockSpec((tm, tn), lambda i, j: (i, j))],
        out_specs=pl.BlockSpec((tm, tn), lambda i, j: (i, j)),
        out_shape=jax.ShapeDtypeStruct((m, n), F32),
        compiler_params=_cparams("parallel", "arbitrary"), name="matmul_residual")(a, w, x)


def _matmul2_residual_body(a_ref, b_ref, wa_ref, wb_ref, x_ref, o_ref):
    o_ref[...] = x_ref[...] + _dot(a_ref[...], wa_ref[...]) + _dot(b_ref[...], wb_ref[...])


def matmul2_residual(a, b, wa, wb, x, *, tm=1024, tn=512):
    m, ka = a.shape
    kb = b.shape[1]
    n = wa.shape[1]
    tm = min(tm, m)
    return pl.pallas_call(
        _matmul2_residual_body, grid=(m // tm, n // tn),
        in_specs=[pl.BlockSpec((tm, ka), lambda i, j: (i, 0)),
                  pl.BlockSpec((tm, kb), lambda i, j: (i, 0)),
                  pl.BlockSpec((ka, tn), lambda i, j: (0, j)),
                  pl.BlockSpec((kb, tn), lambda i, j: (0, j)),
                  pl.BlockSpec((tm, tn), lambda i, j: (i, j))],
        out_specs=pl.BlockSpec((tm, tn), lambda i, j: (i, j)),
        out_shape=jax.ShapeDtypeStruct((m, n), F32),
        compiler_params=_cparams("parallel", "arbitrary"), name="matmul2_residual")(a, b, wa, wb, x)


def _mem_attn_body(heads, mq_ref, mkv_ref, qn_ref, kn_ref, o_ref):
    width = mq_ref.shape[-1]
    dh = width // heads
    scale = dh ** -0.5
    for h in range(heads):
        q = _rms(mq_ref[:, h * dh:(h + 1) * dh].astype(F32), qn_ref[...]) * scale
        k = _rms(mkv_ref[:, h * dh:(h + 1) * dh].astype(F32), kn_ref[...])
        v = mkv_ref[:, width + h * dh:width + (h + 1) * dh]
        s = _dot_nt(q.astype(BF16), k.astype(BF16))
        p = jnp.exp(s - jnp.max(s, axis=-1, keepdims=True))
        p = p / jnp.sum(p, axis=-1, keepdims=True)
        o_ref[:, h * dh:(h + 1) * dh] = _dot(p.astype(BF16), v).astype(o_ref.dtype)


def mem_attention(proj, mkv, q_norm, k_norm, *, width, tm=512):
    b, t, c = proj.shape
    mlen = mkv.shape[1]
    dh = width // MEM_HEADS
    qblk = (c - width) // width
    return pl.pallas_call(
        functools.partial(_mem_attn_body, MEM_HEADS), grid=(b, t // tm),
        in_specs=[pl.BlockSpec((None, tm, width), lambda bi, i: (bi, i, qblk)),
                  pl.BlockSpec((None, mlen, 2 * width), lambda bi, i: (bi, 0, 0)),
                  pl.BlockSpec((1, dh), lambda bi, i: (0, 0)),
                  pl.BlockSpec((1, dh), lambda bi, i: (0, 0))],
        out_specs=pl.BlockSpec((None, tm, width), lambda bi, i: (bi, i, 0)),
        out_shape=jax.ShapeDtypeStruct((b, t, width), BF16),
        compiler_params=_cparams("parallel", "parallel"), name="mem_attention",
    )(proj, mkv, q_norm.reshape(1, dh), k_norm.reshape(1, dh))


def _rope(y, c, s1, s2):
    half = ROPE_DIM // 2
    return y * c + pltpu.roll(y, HEAD_DIM - half, 1) * s1 + pltpu.roll(y, half, 1) * s2


def _nsa_prep_body(groups, q_ref, cmp_ref, sel_ref, win_ref, c_ref, s1_ref, s2_ref, qn_ref, kn_ref,
                   qo_ref, kcv_ref, ksa_ref, vsa_ref, kw_ref, vwa_ref):
    dh = HEAD_DIM
    tt = q_ref.shape[0]
    n_heads = q_ref.shape[1] // dh
    c, s1, s2 = c_ref[...], s1_ref[...], s2_ref[...]
    scale = dh ** -0.5
    qn = qn_ref[...] * scale
    for h in range(n_heads):
        y = _rms(q_ref[:, h * dh:(h + 1) * dh].astype(F32), qn)
        qo_ref[h] = _rope(y, c, s1, s2).astype(BF16)
    t = pl.program_id(1) * tt + lax.broadcasted_iota(jnp.int32, (tt, dh), 0)
    lane = lax.broadcasted_iota(jnp.int32, (tt, dh), 1)
    onehot = jnp.where(lane == _floordiv_pow2(t, SEL_BLOCK), 1.0, 0.0).astype(BF16)
    ones = jnp.ones((tt, dh), BF16)
    for g in range(groups):
        sl = slice(g * dh, (g + 1) * dh)
        vl = slice((groups + g) * dh, (groups + g + 1) * dh)
        lo = slice(2 * g * dh, (2 * g + 1) * dh)
        hi = slice((2 * g + 1) * dh, (2 * g + 2) * dh)
        kc = _rope(_rms(cmp_ref[:, sl].astype(F32), kn_ref[0:1, :]), c, s1, s2)
        kcv_ref[g] = kc.astype(BF16)
        kcv_ref[groups + g] = cmp_ref[:, vl]
        ks = _rope(_rms(sel_ref[:, sl].astype(F32), kn_ref[1:2, :]), c, s1, s2)
        ksa_ref[:, lo] = ks.astype(BF16)
        ksa_ref[:, hi] = onehot
        vsa_ref[:, lo] = sel_ref[:, vl]
        vsa_ref[:, hi] = ones
        kw = _rope(_rms(win_ref[:, sl].astype(F32), kn_ref[2:3, :]), c, s1, s2)
        kw_ref[:, sl] = kw.astype(BF16)
        vwa_ref[:, lo] = win_ref[:, vl]
        vwa_ref[:, hi] = ones


def nsa_prep(proj, rope_c, rope_s1, rope_s2, q_norm, k_norm, *, d_model, tt=256):
    b, t, _ = proj.shape
    g, dh = NSA_KV_GROUPS, HEAD_DIM
    kv2 = 2 * g * dh
    base = d_model // kv2
    tab = pl.BlockSpec((None, tt, dh), lambda bi, i: (bi, i, 0))
    return pl.pallas_call(
        functools.partial(_nsa_prep_body, g), grid=(b, t // tt),
        in_specs=[pl.BlockSpec((None, tt, d_model), lambda bi, i: (bi, i, 0)),
                  pl.BlockSpec((None, tt, kv2), lambda bi, i: (bi, i, base)),
                  pl.BlockSpec((None, tt, kv2), lambda bi, i: (bi, i, base + 1)),
                  pl.BlockSpec((None, tt, kv2), lambda bi, i: (bi, i, base + 2)),
                  tab, tab, tab,
                  pl.BlockSpec((1, dh), lambda bi, i: (0, 0)),
                  pl.BlockSpec((3, dh), lambda bi, i: (0, 0))],
        out_specs=[pl.BlockSpec((None, d_model // dh, tt, dh), lambda bi, i: (bi, 0, i, 0)),
                   pl.BlockSpec((None, 2 * g, tt, dh), lambda bi, i: (bi, 0, i, 0)),
                   pl.BlockSpec((None, tt, 2 * g * dh), lambda bi, i: (bi, i, 0)),
                   pl.BlockSpec((None, tt, 2 * g * dh), lambda bi, i: (bi, i, 0)),
                   pl.BlockSpec((None, tt, g * dh), lambda bi, i: (bi, i, 0)),
                   pl.BlockSpec((None, tt, 2 * g * dh), lambda bi, i: (bi, i, 0))],
        out_shape=[jax.ShapeDtypeStruct((b, d_model // dh, t, dh), BF16),
                   jax.ShapeDtypeStruct((b, 2 * g, t, dh), BF16),
                   jax.ShapeDtypeStruct((b, t, 2 * g * dh), BF16),
                   jax.ShapeDtypeStruct((b, t, 2 * g * dh), BF16),
                   jax.ShapeDtypeStruct((b, t, g * dh), BF16),
                   jax.ShapeDtypeStruct((b, t, 2 * g * dh), BF16)],
        compiler_params=_cparams("parallel", "parallel"), name="nsa_prep",
    )(proj, proj, proj, proj, rope_c, rope_s1, rope_s2, q_norm.reshape(1, dh), k_norm)


def _nsa_compress_body(p_ref, w1_ref, w2_ref, pos_ref, o_ref):
    n_pieces, half = p_ref.shape
    p = p_ref[...]
    u = _dot(p, w1_ref[0:half, :])
    v = _dot(p, w1_ref[half:2 * half, :])
    pos = jnp.broadcast_to(pos_ref[...], (8, 2 * half)).astype(BF16)
    c = _dot(pos, w1_ref[...])[0:1, :]
    hid = u + pltpu.roll(v, n_pieces - 1, 0) + c
    hid = hid * jax.nn.sigmoid(hid)
    o_ref[...] = _dot(hid.astype(BF16), w2_ref[...]).astype(o_ref.dtype)


def nsa_compress(kcv, w1, w2, pos):
    b, g2, n_pieces, half = kcv.shape
    dh = HEAD_DIM
    g = g2 // 2
    return pl.pallas_call(
        _nsa_compress_body, grid=(b, g2),
        in_specs=[pl.BlockSpec((None, None, n_pieces, half), lambda bi, gi: (bi, gi, 0, 0)),
                  pl.BlockSpec((None, 2 * half, dh), lambda bi, gi: (gi // g, 0, 0)),
                  pl.BlockSpec((None, dh, dh), lambda bi, gi: (gi // g, 0, 0)),
                  pl.BlockSpec((None, 1, 2 * half), lambda bi, gi: (gi // g, 0, 0))],
        out_specs=pl.BlockSpec((None, None, n_pieces, dh), lambda bi, gi: (bi, gi, 0, 0)),
        out_shape=jax.ShapeDtypeStruct((b, g2, n_pieces, dh), BF16),
        compiler_params=_cparams("parallel", "parallel"), name="nsa_compress",
    )(kcv, w1, w2, pos)


def _nsa_attn_body(tk, q_ref, gate_ref, kc_ref, vc_ref, ksa_ref, vsa_ref, kw_ref, vwa_ref, o_ref, mx_ref, acc_ref, s_ref):
    dh = HEAD_DIM
    hpg, tq, _ = q_ref.shape
    rows = hpg * tq
    n_cmp = kc_ref.shape[0]
    q0 = pl.program_id(2) * tq
    q_all = q_ref[...].reshape(rows, dh)
    t_one = q0 + lax.broadcasted_iota(jnp.int32, (tq, 1), 0)
    t_col = q0 + (lax.broadcasted_iota(jnp.int32, (rows, 1), 0) & (tq - 1))

    n_row = lax.broadcasted_iota(jnp.int32, (1, n_cmp), 1)
    cbias = jnp.where((n_row * CMP_STRIDE + (CMP_BLOCK - 1)) <= t_one, 0.0, -BIG)
    s_cmp = (_dot_nt(q_all, kc_ref[...]).reshape(hpg, tq, n_cmp) + cbias[None]).reshape(rows, n_cmp)
    p_cmp = jnp.exp(s_cmp - jnp.max(s_cmp, axis=-1, keepdims=True))
    inv = jnp.where(t_col >= CMP_BLOCK - 1, 1.0 / jnp.sum(p_cmp, axis=-1, keepdims=True), 0.0)
    p_cmp = p_cmp * inv
    o_cmp = _dot(p_cmp.astype(BF16), vc_ref[...])
    imp = jnp.sum(p_cmp.reshape(hpg, tq, n_cmp), axis=0)

    rs = SEL_BLOCK // CMP_STRIDE
    n_sel = ksa_ref.shape[0] // SEL_BLOCK
    mm = lax.broadcasted_iota(jnp.int32, (n_sel, n_cmp), 0)
    nn = lax.broadcasted_iota(jnp.int32, (n_sel, n_cmp), 1)
    pool_t = jnp.where((nn >= rs * mm - (CMP_BLOCK // CMP_STRIDE - 1)) & (nn <= rs * mm + rs - 1), 1.0, 0.0).astype(BF16)
    imp_hi = imp.astype(BF16)
    imp_lo = (imp - imp_hi.astype(F32)).astype(BF16)
    imp_sel = _dot_nt(pool_t, imp_hi) + _dot_nt(pool_t, imp_lo)

    blk = lax.broadcasted_iota(jnp.int32, (n_sel, tq), 0)
    cur = _floordiv_pow2(q0 + lax.broadcasted_iota(jnp.int32, (n_sel, tq), 1), SEL_BLOCK)
    forced = (blk == 0) | (blk == cur) | (blk == cur - 1)
    valid = blk <= cur
    score = jnp.where(valid, jnp.where(forced, BIG, imp_sel), -BIG)
    sub = 8
    groups = [score[g * sub:(g + 1) * sub, :] for g in range(n_sel // sub)]
    ranks = [jnp.zeros((sub, tq), F32) for _ in groups]
    sub_iota = lax.broadcasted_iota(jnp.int32, (sub, tq), 0)
    for m in range(n_sel):
        row = jnp.broadcast_to(score[m:m + 1, :], (sub, tq))
        for g, sg in enumerate(groups):
            if g * sub > m:
                inc = jnp.where(row >= sg, 1.0, 0.0)
            elif g * sub + sub - 1 <= m:
                inc = jnp.where(row > sg, 1.0, 0.0)
            else:
                inc = jnp.where(sub_iota + g * sub > m, jnp.where(row >= sg, 1.0, 0.0), jnp.where(row > sg, 1.0, 0.0))
            ranks[g] = ranks[g] + inc
    rank = jnp.concatenate(ranks, axis=0)
    selected = valid & (rank < float(min(SEL_TOPK, n_sel)))
    bias_t = jnp.where(selected, 0.0, SEL_MASK_BIAS)
    if n_sel < LANES:
        bias_t = jnp.concatenate([bias_t, jnp.zeros((LANES - n_sel, tq), F32)], axis=0)
    sel_bias = bias_t.T.astype(BF16)

    n_split = 2
    hr = rows // n_split
    q_parts = [q_all[i * hr:(i + 1) * hr] for i in range(n_split)]
    n_full = q0 // tk
    d_start = pl.multiple_of(n_full * tk, tk)

    def scores(ks):
        k = ksa_ref[pl.ds(ks, tk), :]
        bias = _dot_nt(sel_bias, k[:, dh:2 * dh])
        return [(_dot_nt(qp, k[:, 0:dh]).reshape(hpg // n_split, tq, tk) + bias[None]).reshape(hr, tk) for qp in q_parts]

    def fold(s):
        out = s[:, 0:LANES]
        for c in range(1, tk // LANES):
            out = jnp.maximum(out, s[:, c * LANES:(c + 1) * LANES])
        return out

    dmask = (d_start + lax.broadcasted_iota(jnp.int32, (1, tk), 1)) <= t_col[0:hr]
    for i, s in enumerate(scores(d_start)):
        r = slice(i * hr, (i + 1) * hr)
        s = jnp.where(dmask, s, -BIG)
        s_ref[n_full, r, :] = s
        mx_ref[r, :] = fold(s)

    def max_tiles(kts):
        all_s = [scores(pl.multiple_of(kt * tk, tk)) for kt in kts]
        for kt, parts in zip(kts, all_s):
            for i, s in enumerate(parts):
                r = slice(i * hr, (i + 1) * hr)
                s_ref[kt, r, :] = s
                mx_ref[r, :] = jnp.maximum(mx_ref[r, :], fold(s))

    def max_step(r, carry):
        max_tiles([2 * r, 2 * r + 1])
        return carry

    lax.fori_loop(0, n_full // 2, max_step, 0)

    @pl.when((n_full & 1) == 1)
    def _():
        max_tiles([n_full - 1])

    m_row = jnp.broadcast_to(jnp.max(mx_ref[...], axis=-1, keepdims=True), (rows, LANES))
    m_row = jnp.concatenate([m_row] * (tk // LANES), axis=-1)
    m_parts = [m_row[i * hr:(i + 1) * hr] for i in range(n_split)]

    acc_ref[...] = jnp.zeros_like(acc_ref)

    def acc_tiles(kts):
        ps = [[jnp.exp(s_ref[kt, i * hr:(i + 1) * hr, :] - m_parts[i]).astype(BF16) for i in range(n_split)] for kt in kts]
        for kt, parts in zip(kts, ps):
            v = vsa_ref[pl.ds(pl.multiple_of(kt * tk, tk), tk), :]
            for i, p in enumerate(parts):
                r = slice(i * hr, (i + 1) * hr)
                acc_ref[r, :] += _dot(p, v)

    def acc_step(r, carry):
        acc_tiles([2 * r, 2 * r + 1])
        return carry

    n_tiles = n_full + 1
    lax.fori_loop(0, n_tiles // 2, acc_step, 0)

    @pl.when((n_tiles & 1) == 1)
    def _():
        acc_tiles([n_tiles - 1])

    acc = acc_ref[...]
    o_sel = acc[:, 0:dh] / acc[:, dh:2 * dh]

    w_len = WINDOW + tq
    w_start = pl.multiple_of(jnp.maximum(q0 - WINDOW, 0), tq)
    diff = t_one - (w_start + lax.broadcasted_iota(jnp.int32, (1, w_len), 1))
    wbias = jnp.where((diff >= 0) & (diff < WINDOW), 0.0, -BIG)
    s_win = (_dot_nt(q_all, kw_ref[pl.ds(w_start, w_len), :]).reshape(hpg, tq, w_len) + wbias[None]).reshape(rows, w_len)
    p_win = jnp.exp(s_win - jnp.max(s_win, axis=-1, keepdims=True))
    acc_w = _dot(p_win.astype(BF16), vwa_ref[pl.ds(w_start, w_len), :])
    o_win = acc_w[:, 0:dh] / acc_w[:, dh:2 * dh]

    for j in range(hpg):
        r = slice(j * tq, (j + 1) * tq)
        g_cmp = gate_ref[:, j:j + 1]
        g_sel = gate_ref[:, hpg + j:hpg + j + 1]
        g_win = gate_ref[:, 2 * hpg + j:2 * hpg + j + 1]
        o_ref[:, j * dh:(j + 1) * dh] = (g_cmp * o_cmp[r] + g_sel * o_sel[r] + g_win * o_win[r]).astype(o_ref.dtype)


def nsa_attention(qh, gates, kv_cmp, ks_aug, vs_aug, kw, vw_aug, *, d_model, tq=128, tk=256):
    b, _, t, dh = qh.shape
    g = NSA_KV_GROUPS
    hpg = d_model // dh // g
    n_cmp = kv_cmp.shape[2]
    aug = pl.BlockSpec((None, t, 2 * dh), lambda bi, gi, i: (bi, 0, gi))
    return pl.pallas_call(
        functools.partial(_nsa_attn_body, tk), grid=(b, g, t // tq),
        in_specs=[pl.BlockSpec((None, hpg, tq, dh), lambda bi, gi, i: (bi, gi, i, 0)),
                  pl.BlockSpec((None, tq, LANES), lambda bi, gi, i: (bi, i, gi)),
                  pl.BlockSpec((None, None, n_cmp, dh), lambda bi, gi, i: (bi, gi, 0, 0)),
                  pl.BlockSpec((None, None, n_cmp, dh), lambda bi, gi, i: (bi, g + gi, 0, 0)),
                  aug, aug,
                  pl.BlockSpec((None, t, dh), lambda bi, gi, i: (bi, 0, gi)),
                  aug],
        out_specs=pl.BlockSpec((None, tq, hpg * dh), lambda bi, gi, i: (bi, i, gi)),
        out_shape=jax.ShapeDtypeStruct((b, t, d_model), BF16),
        scratch_shapes=[pltpu.VMEM((hpg * tq, LANES), F32), pltpu.VMEM((hpg * tq, 2 * dh), F32),
                        pltpu.VMEM((t // tk, hpg * tq, tk), F32)],
        compiler_params=_cparams("parallel", "parallel", "arbitrary"), name="nsa_attention",
    )(qh, gates, kv_cmp, kv_cmp, ks_aug, vs_aug, kw, vw_aug)


def nsa_mixer(proj, gates, rope_tabs, q_norm, k_norm, cmp_pos, cmp_w1, cmp_w2, *, d_model):
    b, t, _ = proj.shape
    dh = HEAD_DIM
    qh, kcv, ks_aug, vs_aug, kw, vw_aug = nsa_prep(proj, *rope_tabs, q_norm, k_norm, d_model=d_model)
    n_pieces = t // CMP_STRIDE
    kcv = kcv.reshape(b, 2 * NSA_KV_GROUPS, n_pieces, CMP_STRIDE * dh)
    kv_cmp = nsa_compress(kcv, cmp_w1.astype(BF16), cmp_w2.astype(BF16), cmp_pos.reshape(2, 1, CMP_BLOCK * dh))
    return nsa_attention(qh, gates.reshape(b, t, NSA_KV_GROUPS * LANES), kv_cmp, ks_aug, vs_aug, kw, vw_aug, d_model=d_model)


def _conv_body(halo, cw, p_ref, ph_ref, bin_ref, w_ref, wb_ref, lg_ref, lb_ref, o_ref, u_ref, y_ref, sh_ref):
    tt, d = o_ref.shape
    i = pl.program_id(1)

    def glu(ref):
        a = ref[:, 0:d].astype(F32) + bin_ref[:, 0:d]
        gate = ref[:, d:2 * d].astype(F32) + bin_ref[:, d:2 * d]
        return a * jax.nn.sigmoid(gate)

    u_ref[halo:halo + tt, :] = glu(p_ref)

    @pl.when(i == 0)
    def _():
        u_ref[0:halo, :] = jnp.zeros((halo, d), F32)

    @pl.when(i > 0)
    def _():
        u_ref[0:halo, :] = glu(ph_ref)

    off = halo - (CONV_WIDTH - 1)
    for c0 in range(0, d, cw):
        cols = slice(c0, c0 + cw)
        for c in range(SUBLANES):
            rows_c = tt + SUBLANES * ((CONV_WIDTH - 1 - c) // SUBLANES)
            sh_ref[c, 0:rows_c, :] = u_ref[off + c:off + c + rows_c, cols]
        acc = jnp.zeros((tt, cw), F32) + wb_ref[:, cols]
        for k in range(CONV_WIDTH):
            b, c = divmod(k, SUBLANES)
            acc = acc + w_ref[k:k + 1, cols] * sh_ref[c, SUBLANES * b:SUBLANES * b + tt, :]
        y_ref[:, cols] = acc

    y = y_ref[...]
    yc = y - jnp.mean(y, axis=-1, keepdims=True)
    yn = yc * lax.rsqrt(jnp.mean(yc * yc, axis=-1, keepdims=True) + NORM_EPS) * lg_ref[...] + lb_ref[...]
    o_ref[...] = (yn * jax.nn.sigmoid(yn)).astype(o_ref.dtype)


def conv_mixer(proj, b_in, dw_w, dw_b, ln_g, ln_b, *, d_model, tt=256, halo=32, cw=256):
    b, t, _ = proj.shape
    d = d_model
    hb = tt // halo
    row = lambda v: v.reshape(1, -1)
    const = lambda shape: pl.BlockSpec(shape, lambda bi, i: (0, 0))
    return pl.pallas_call(
        functools.partial(_conv_body, halo, cw), grid=(b, t // tt),
        in_specs=[pl.BlockSpec((None, tt, 2 * d), lambda bi, i: (bi, i, 0)),
                  pl.BlockSpec((None, halo, 2 * d), lambda bi, i: (bi, jnp.maximum(i * hb - 1, 0), 0)),
                  const((1, 2 * d)), const((CONV_WIDTH, d)), const((1, d)), const((1, d)), const((1, d))],
        out_specs=pl.BlockSpec((None, tt, d), lambda bi, i: (bi, i, 0)),
        out_shape=jax.ShapeDtypeStruct((b, t, d), BF16),
        scratch_shapes=[pltpu.VMEM((halo + tt, d), F32), pltpu.VMEM((tt, d), F32),
                        pltpu.VMEM((SUBLANES, tt + SUBLANES * ((CONV_WIDTH - 1) // SUBLANES), cw), F32)],
        compiler_params=_cparams("parallel", "arbitrary"), name="conv_mixer",
    )(proj, proj, row(b_in), dw_w, row(dw_b), row(ln_g), row(ln_b))


def _sb_body(tk, q_ref, k_ref, v_ref, o_ref):
    tq = q_ref.shape[0]
    dh = HEAD_DIM
    hp = q_ref.shape[1] // dh
    heads = [slice(h * dh, (h + 1) * dh) for h in range(hp)]
    q0 = pl.program_id(2) * tq
    ss = lax.broadcasted_iota(jnp.int32, (tk, tk), 0)
    jj = lax.broadcasted_iota(jnp.int32, (tk, tk), 1)
    suffix = jnp.where(ss > jj, 1.0, 0.0).astype(BF16)

    def tiles(ks, carries, strict):
        zs = [_dot_nt(q_ref[:, hs], k_ref[pl.ds(ks, tk), hs]) for hs in heads]
        log_betas, log_keeps = [], []
        for z in zs:
            neg_abs = lax.bitcast_convert_type(lax.bitcast_convert_type(z, jnp.uint32) | jnp.uint32(0x80000000), F32)
            log_beta = jnp.minimum(z, 0.0) - jnp.log2(1.0 + jnp.exp2(neg_abs))
            log_keep = log_beta - z
            if strict is not None:
                log_keep = jnp.where(strict, log_keep, 0.0)
            log_betas.append(log_beta)
            log_keeps.append(log_keep)
        afters = [_dot(lk.astype(BF16), suffix) for lk in log_keeps]
        out = []
        for h in range(hp):
            c_run, acc = carries[h]
            a = jnp.exp2(log_betas[h] + afters[h] + c_run)
            if strict is not None:
                a = jnp.where(strict, a, 0.0)
            acc = acc + _dot(a.astype(BF16), v_ref[pl.ds(ks, tk), heads[h]])
            c_run = c_run + jnp.sum(log_keeps[h], axis=-1, keepdims=True)
            out.append((c_run, acc))
        return tuple(out)

    carries = tuple((jnp.zeros((tq, 1), F32), jnp.zeros((tq, dh), F32)) for _ in range(hp))
    t_col = q0 + lax.broadcasted_iota(jnp.int32, (tq, 1), 0)
    for d in range(tq // tk - 1, -1, -1):
        ks = pl.multiple_of(q0 + d * tk, tk)
        strict = (ks + lax.broadcasted_iota(jnp.int32, (1, tk), 1)) < t_col
        carries = tiles(ks, carries, strict)
    n_full = q0 // tk

    def step(r, carries):
        return tiles(pl.multiple_of((n_full - 1 - r) * tk, tk), carries, None)

    carries = lax.fori_loop(0, n_full, step, carries)
    for h in range(hp):
        o_ref[:, heads[h]] = carries[h][1].astype(o_ref.dtype)


def sb_mixer(proj, *, d_model, tq=256, tk=256, hp=4):
    b, t, _ = proj.shape
    w = hp * HEAD_DIM
    nb = d_model // w
    return pl.pallas_call(
        functools.partial(_sb_body, tk), grid=(b, nb, t // tq),
        in_specs=[pl.BlockSpec((None, tq, w), lambda bi, h, i: (bi, i, h)),
                  pl.BlockSpec((None, t, w), lambda bi, h, i: (bi, 0, nb + h)),
                  pl.BlockSpec((None, t, w), lambda bi, h, i: (bi, 0, 2 * nb + h))],
        out_specs=pl.BlockSpec((None, tq, w), lambda bi, h, i: (bi, i, h)),
        out_shape=jax.ShapeDtypeStruct((b, t, d_model), BF16),
        compiler_params=_cparams("parallel", "parallel", "arbitrary"), name="sb_mixer",
    )(proj, proj, proj)


def _gelu(x):
    return 0.5 * x * (1.0 + lax.erf(x * (2.0 ** -0.5)))


def _gmlp_body(p_ref, lg_ref, lb_ref, ws_ref, bs_ref, o_ref):
    tt, d = o_ref.shape
    gc = ws_ref.shape[1]
    v = _gelu(p_ref[:, d:2 * d].astype(F32))
    vc = v - jnp.mean(v, axis=-1, keepdims=True)
    vn = (vc * lax.rsqrt(jnp.mean(vc * vc, axis=-1, keepdims=True) + NORM_EPS) * lg_ref[...] + lb_ref[...]).astype(BF16)
    causal = lax.broadcasted_iota(jnp.int32, (gc, gc), 0) >= lax.broadcasted_iota(jnp.int32, (gc, gc), 1)
    for g in range(d // gc):
        sl = slice(g * gc, (g + 1) * gc)
        ws = jnp.where(causal, ws_ref[g], 0.0).astype(BF16)
        mixed = _dot(ws, vn[:, sl]) + bs_ref[:, sl]
        o_ref[:, sl] = (_gelu(p_ref[:, sl].astype(F32)) * mixed).astype(o_ref.dtype)


def gmlp_mixer(proj, ln_g, ln_b, w_s, b_s, *, d_model):
    b, t, _ = proj.shape
    d = d_model
    n_groups, gc, _ = w_s.shape
    row = lambda v: v.reshape(1, -1)
    const = lambda shape: pl.BlockSpec(shape, lambda bi, i: (0,) * len(shape))
    bs_cols = jnp.repeat(b_s.T, d // n_groups, axis=1)
    return pl.pallas_call(
        _gmlp_body, grid=(b, t // GM_CHUNK),
        in_specs=[pl.BlockSpec((None, GM_CHUNK, 2 * d), lambda bi, i: (bi, i, 0)),
                  const((1, d)), const((1, d)), const((n_groups, gc, gc)), const((GM_CHUNK, d))],
        out_specs=pl.BlockSpec((None, GM_CHUNK, d), lambda bi, i: (bi, i, 0)),
        out_shape=jax.ShapeDtypeStruct((b, t, d), BF16),
        compiler_params=_cparams("parallel", "parallel"), name="gmlp_mixer",
    )(proj, row(ln_g), row(ln_b), w_s, bs_cols)


def _rope_tables(positions):
    half = ROPE_DIM // 2
    inv = 1.0 / (ROPE_THETA ** (jnp.arange(0, ROPE_DIM, 2, dtype=F32) / ROPE_DIM))
    ang = positions.astype(F32)[..., None] * inv
    cos, sin = jnp.cos(ang), jnp.sin(ang)
    zeros = jnp.zeros_like(cos)
    tail = jnp.zeros(cos.shape[:-1] + (HEAD_DIM - ROPE_DIM,), F32)
    c = jnp.concatenate([cos, cos, tail + 1.0], axis=-1)
    s1 = jnp.concatenate([-sin, zeros, tail], axis=-1)
    s2 = jnp.concatenate([zeros, sin, tail], axis=-1)
    return c, s1, s2


def _pad_to(w, axis, mult):
    pad = (-w.shape[axis]) % mult
    if pad == 0:
        return w
    widths = [(0, 0)] * w.ndim
    widths[axis] = (0, pad)
    return jnp.pad(w, widths)


def _cast_pad_halves_body(n_in, w_ref, o_ref):
    n_out = o_ref.shape[1] // 2
    for half in range(2):
        o_ref[:, half * n_out:half * n_out + n_in] = w_ref[:, half * n_in:(half + 1) * n_in].astype(o_ref.dtype)
        if n_out > n_in:
            o_ref[:, half * n_out + n_in:(half + 1) * n_out] = jnp.zeros((o_ref.shape[0], n_out - n_in), o_ref.dtype)


def cast_pad_halves(w, n_in, mult, *, tr=64):
    k = w.shape[0]
    n_out = n_in + (-n_in) % mult
    return pl.pallas_call(
        functools.partial(_cast_pad_halves_body, n_in), grid=(k // tr,),
        in_specs=[pl.BlockSpec((tr, 2 * n_in), lambda i: (i, 0))],
        out_specs=pl.BlockSpec((tr, 2 * n_out), lambda i: (i, 0)),
        out_shape=jax.ShapeDtypeStruct((k, 2 * n_out), BF16),
        compiler_params=_cparams("parallel"), name="cast_pad_halves")(w)


def _ffn(x2, norm_g, w_gu, w_down, d_ff, tn=512):
    w_gu = cast_pad_halves(w_gu, d_ff, tn)
    act = norm_swiglu(x2, norm_g, w_gu, tn=tn)
    return matmul_residual(act, w_down.astype(BF16), x2, 0.5)


def kernel(x, mem, positions, ffn1_norm, ffn1_w_gu, ffn1_w_down, mix_norm, ffn2_norm, ffn2_w_gu, ffn2_w_down, mem_norm, mem_w_kv, mem_k_norm, mem_q_norm, nsa_w_in, nsa_q_norm, nsa_k_norm, nsa_cmp_pos, nsa_cmp_w1, nsa_cmp_w2, nsa_w_out, conv_w_in, conv_b_in, conv_dw_w, conv_dw_b, conv_ln_g, conv_ln_b, conv_w_out, sb_w_in, sb_w_out, gm_w_in, gm_ln_g, gm_ln_b, gm_ws, gm_bs, gm_w_out):
    b, t, d = x.shape
    depth = ffn1_norm.shape[0]
    d_ff = ffn1_w_down.shape[1]
    mem_len = mem.shape[1]
    mem_width = mem_w_kv.shape[1] // 2
    n_heads = d // HEAD_DIM
    n_mixers = 4

    rope_tabs = _rope_tables(positions)
    mkv = norm_matmul(mem.reshape(b * mem_len, d), mem_norm, mem_w_kv.astype(BF16)).reshape(b, mem_len, 2 * mem_width)

    x2 = x.reshape(b * t, d)
    for i in range(depth):
        kind, j = i % n_mixers, i // n_mixers
        x2 = _ffn(x2, ffn1_norm[i], ffn1_w_gu[i], ffn1_w_down[i], d_ff)
        if kind == 0:
            w_in = nsa_w_in[j]
            n_main = w_in.shape[1] - mem_width - 3 * n_heads
            hpg = n_heads // NSA_KV_GROUPS
            w_gate = w_in[:, n_main:n_main + 3 * n_heads].reshape(d, NSA_KV_GROUPS, hpg, 3).transpose(0, 1, 3, 2)
            w_gate = _pad_to(w_gate.reshape(d, NSA_KV_GROUPS, 3 * hpg).astype(BF16), 2, LANES).reshape(d, -1)
            w_main = jnp.concatenate([w_in[:, :n_main], w_in[:, -mem_width:]], axis=1).astype(BF16)
            proj, gates = norm_matmul(x2, mix_norm[i], w_main, side_w=w_gate)
            proj = proj.reshape(b, t, -1)
            mixed = nsa_mixer(proj, gates, rope_tabs, nsa_q_norm[j], nsa_k_norm[j], nsa_cmp_pos[j],
                              nsa_cmp_w1[j], nsa_cmp_w2[j], d_model=d)
            w_out = nsa_w_out[j]
        elif kind == 1:
            proj = norm_matmul(x2, mix_norm[i], conv_w_in[j].astype(BF16)).reshape(b, t, -1)
            mixed = conv_mixer(proj, conv_b_in[j], conv_dw_w[j], conv_dw_b[j], conv_ln_g[j], conv_ln_b[j], d_model=d)
            w_out = conv_w_out[j]
        elif kind == 2:
            w_in = sb_w_in[j]
            w_in = jnp.concatenate([w_in[:, :d] * SB_SCORE_SCALE, w_in[:, d:]], axis=1).astype(BF16)
            proj = norm_matmul(x2, mix_norm[i], w_in).reshape(b, t, -1)
            mixed = sb_mixer(proj, d_model=d)
            w_out = sb_w_out[j]
        else:
            proj = norm_matmul(x2, mix_norm[i], gm_w_in[j].astype(BF16)).reshape(b, t, -1)
            mixed = gmlp_mixer(proj, gm_ln_g[j], gm_ln_b[j], gm_ws[j], gm_bs[j], d_model=d)
            w_out = gm_w_out[j]
        mem_out = mem_attention(proj, mkv, mem_q_norm[i], mem_k_norm, width=mem_width)
        w_out = w_out.astype(BF16)
        x2 = matmul2_residual(mixed.reshape(b * t, d), mem_out.reshape(b * t, mem_width), w_out[:d], w_out[d:], x2)
        x2 = _ffn(x2, ffn2_norm[i], ffn2_w_gu[i], ffn2_w_down[i], d_ff)
    return x2.reshape(b, t, d)
```

```python
import functools

import jax
import jax.numpy as jnp
from jax import lax
from jax.experimental import pallas as pl
from jax.experimental.pallas import tpu as pltpu

F32 = jnp.float32
BF16 = jnp.bfloat16

HEAD_DIM = 128
ROPE_THETA = 500000.0
ROPE_DIM = HEAD_DIM // 4
NORM_EPS = 1e-6
BIG = 1e30
NSA_KV_GROUPS = 4
CMP_BLOCK = 32
CMP_STRIDE = 16
SEL_BLOCK = 64
SEL_TOPK = 16
WINDOW = 512
CONV_WIDTH = 31
GM_CHUNK = 128
MEM_HEADS = 4

LANES = 128
SUBLANES = 8
VMEM_LIMIT_BYTES = 56 * 1024 * 1024
SEL_MASK_BIAS = -32768.0
SB_SCORE_SCALE = HEAD_DIM ** -0.5 * 1.4426950408889634


def _cparams(*sem):
    return pltpu.CompilerParams(dimension_semantics=sem, vmem_limit_bytes=VMEM_LIMIT_BYTES)


def _dot(a, b):
    return jnp.dot(a, b, preferred_element_type=F32)


def _dot_nt(a, b):
    return lax.dot_general(a, b, (((1,), (1,)), ((), ())), preferred_element_type=F32)


def _floordiv_pow2(x, n):
    return lax.shift_right_logical(x, n.bit_length() - 1)


def _rms(x, g):
    return x * lax.rsqrt(jnp.mean(x * x, axis=-1, keepdims=True) + NORM_EPS) * g


def _norm_matmul_body(x_ref, g_ref, w_ref, o_ref, h_ref):
    @pl.when(pl.program_id(1) == 0)
    def _():
        h_ref[...] = _rms(x_ref[...], g_ref[...]).astype(BF16)

    o_ref[...] = _dot(h_ref[...], w_ref[...]).astype(o_ref.dtype)


def _norm_matmul_side_body(x_ref, g_ref, w_ref, ws_ref, o_ref, os_ref, h_ref):
    @pl.when(pl.program_id(1) == 0)
    def _():
        h = _rms(x_ref[...], g_ref[...]).astype(BF16)
        h_ref[...] = h
        os_ref[...] = jax.nn.sigmoid(_dot(h, ws_ref[...]))

    o_ref[...] = _dot(h_ref[...], w_ref[...]).astype(o_ref.dtype)


def norm_matmul(x, g, w, *, tm=512, tn=512, side_w=None):
    m, k = x.shape
    n = w.shape[1]
    tm = min(tm, m)
    grid = (m // tm, n // tn)
    x_spec = pl.BlockSpec((tm, k), lambda i, j: (i, 0))
    g_spec = pl.BlockSpec((1, k), lambda i, j: (0, 0))
    w_spec = pl.BlockSpec((k, tn), lambda i, j: (0, j))
    o_spec = pl.BlockSpec((tm, tn), lambda i, j: (i, j))
    scratch = [pltpu.VMEM((tm, k), BF16)]
    if side_w is None:
        return pl.pallas_call(
            _norm_matmul_body, grid=grid, in_specs=[x_spec, g_spec, w_spec], out_specs=o_spec,
            out_shape=jax.ShapeDtypeStruct((m, n), BF16), scratch_shapes=scratch,
            compiler_params=_cparams("parallel", "arbitrary"), name="norm_matmul")(x, g.reshape(1, k), w)
    ns = side_w.shape[1]
    return pl.pallas_call(
        _norm_matmul_side_body, grid=grid,
        in_specs=[x_spec, g_spec, w_spec, pl.BlockSpec((k, ns), lambda i, j: (0, 0))],
        out_specs=[o_spec, pl.BlockSpec((tm, ns), lambda i, j: (i, 0))],
        out_shape=[jax.ShapeDtypeStruct((m, n), BF16), jax.ShapeDtypeStruct((m, ns), F32)],
        scratch_shapes=scratch,
        compiler_params=_cparams("parallel", "arbitrary"), name="norm_matmul_side")(x, g.reshape(1, k), w, side_w)


def _norm_swiglu_body(x_ref, g_ref, wg_ref, wu_ref, o_ref, h_ref):
    @pl.when(pl.program_id(1) == 0)
    def _():
        h_ref[...] = _rms(x_ref[...], g_ref[...]).astype(BF16)

    h = h_ref[...]
    gate = _dot(h, wg_ref[...])
    up = _dot(h, wu_ref[...])
    o_ref[...] = (gate * jax.nn.sigmoid(gate) * up).astype(o_ref.dtype)


def norm_swiglu(x, g, w_gu, layer, *, tm=512, tn=512):
    m, k = x.shape
    n = w_gu.shape[2] // 2
    tm = min(tm, m)
    nt = n // tn
    return pl.pallas_call(
        _norm_swiglu_body, grid=(m // tm, nt),
        in_specs=[pl.BlockSpec((tm, k), lambda i, j: (i, 0)),
                  pl.BlockSpec((1, k), lambda i, j: (0, 0)),
                  pl.BlockSpec((None, k, tn), lambda i, j: (layer, 0, j)),
                  pl.BlockSpec((None, k, tn), lambda i, j: (layer, 0, nt + j))],
        out_specs=pl.BlockSpec((tm, tn), lambda i, j: (i, j)),
        out_shape=jax.ShapeDtypeStruct((m, n), BF16),
        scratch_shapes=[pltpu.VMEM((tm, k), BF16)],
        compiler_params=_cparams("parallel", "arbitrary"), name="norm_swiglu")(x, g.reshape(1, k), w_gu, w_gu)


def _matmul_residual_body(scale, a_ref, w_ref, x_ref, o_ref):
    o_ref[...] = x_ref[...] + scale * _dot(a_ref[...], w_ref[...])


def matmul_residual(a, w, layer, x, scale, *, tm=1024, tn=512):
    m = a.shape[0]
    _, k, n = w.shape
    tm = min(tm, m)
    return pl.pallas_call(
        functools.partial(_matmul_residual_body, scale), grid=(m // tm, n // tn),
        in_specs=[pl.BlockSpec((tm, k), lambda i, j: (i, 0)),
                  pl.BlockSpec((None, k, tn), lambda i, j: (layer, 0, j)),
                  pl.BlockSpec((tm, tn), lambda i, j: (i, j))],
        out_specs=pl.BlockSpec((tm, tn), lambda i, j: (i, j)),
        out_shape=jax.ShapeDtypeStruct((m, n), F32),
        compiler_params=_cparams("parallel", "arbitrary"), name="matmul_residual")(a, w, x)


def _matmul2_residual_body(a_ref, b_ref, wa_ref, wb_ref, x_ref, o_ref):
    o_ref[...] = x_ref[...] + _dot(a_ref[...], wa_ref[...]) + _dot(b_ref[...], wb_ref[...])


def matmul2_residual(a, b, wa, wb, x, *, tm=1024, tn=512):
    m, ka = a.shape
    kb = b.shape[1]
    n = wa.shape[1]
    tm = min(tm, m)
    return pl.pallas_call(
        _matmul2_residual_body, grid=(m // tm, n // tn),
        in_specs=[pl.BlockSpec((tm, ka), lambda i, j: (i, 0)),
                  pl.BlockSpec((tm, kb), lambda i, j: (i, 0)),
                  pl.BlockSpec((ka, tn), lambda i, j: (0, j)),
                  pl.BlockSpec((kb, tn), lambda i, j: (0, j)),
                  pl.BlockSpec((tm, tn), lambda i, j: (i, j))],
        out_specs=pl.BlockSpec((tm, tn), lambda i, j: (i, j)),
        out_shape=jax.ShapeDtypeStruct((m, n), F32),
        compiler_params=_cparams("parallel", "arbitrary"), name="matmul2_residual")(a, b, wa, wb, x)


def _mem_attn_body(heads, mq_ref, mkv_ref, qn_ref, kn_ref, o_ref):
    width = mq_ref.shape[-1]
    dh = width // heads
    scale = dh ** -0.5
    for h in range(heads):
        q = _rms(mq_ref[:, h * dh:(h + 1) * dh].astype(F32), qn_ref[...]) * scale
        k = _rms(mkv_ref[:, h * dh:(h + 1) * dh].astype(F32), kn_ref[...])
        v = mkv_ref[:, width + h * dh:width + (h + 1) * dh]
        s = _dot_nt(q.astype(BF16), k.astype(BF16))
        p = jnp.exp(s - jnp.max(s, axis=-1, keepdims=True))
        p = p / jnp.sum(p, axis=-1, keepdims=True)
        o_ref[:, h * dh:(h + 1) * dh] = _dot(p.astype(BF16), v).astype(o_ref.dtype)


def mem_attention(proj, mkv, q_norm, k_norm, *, width, tm=512):
    b, t, c = proj.shape
    mlen = mkv.shape[1]
    dh = width // MEM_HEADS
    qblk = (c - width) // width
    return pl.pallas_call(
        functools.partial(_mem_attn_body, MEM_HEADS), grid=(b, t // tm),
        in_specs=[pl.BlockSpec((None, tm, width), lambda bi, i: (bi, i, qblk)),
                  pl.BlockSpec((None, mlen, 2 * width), lambda bi, i: (bi, 0, 0)),
                  pl.BlockSpec((1, dh), lambda bi, i: (0, 0)),
                  pl.BlockSpec((1, dh), lambda bi, i: (0, 0))],
        out_specs=pl.BlockSpec((None, tm, width), lambda bi, i: (bi, i, 0)),
        out_shape=jax.ShapeDtypeStruct((b, t, width), BF16),
        compiler_params=_cparams("parallel", "parallel"), name="mem_attention",
    )(proj, mkv, q_norm.reshape(1, dh), k_norm.reshape(1, dh))


def _rope(y, c, s1, s2):
    half = ROPE_DIM // 2
    return y * c + pltpu.roll(y, HEAD_DIM - half, 1) * s1 + pltpu.roll(y, half, 1) * s2


def _nsa_prep_body(groups, q_ref, cmp_ref, sel_ref, win_ref, c_ref, s1_ref, s2_ref, qn_ref, kn_ref,
                   qo_ref, kcv_ref, ksa_ref, vsa_ref, kw_ref, vwa_ref):
    dh = HEAD_DIM
    tt = q_ref.shape[0]
    n_heads = q_ref.shape[1] // dh
    c, s1, s2 = c_ref[...], s1_ref[...], s2_ref[...]
    scale = dh ** -0.5
    qn = qn_ref[...] * scale
    for h in range(n_heads):
        y = _rms(q_ref[:, h * dh:(h + 1) * dh].astype(F32), qn)
        qo_ref[h] = _rope(y, c, s1, s2).astype(BF16)
    t = pl.program_id(1) * tt + lax.broadcasted_iota(jnp.int32, (tt, dh), 0)
    lane = lax.broadcasted_iota(jnp.int32, (tt, dh), 1)
    onehot = jnp.where(lane == _floordiv_pow2(t, SEL_BLOCK), 1.0, 0.0).astype(BF16)
    ones = jnp.ones((tt, dh), BF16)
    for g in range(groups):
        sl = slice(g * dh, (g + 1) * dh)
        vl = slice((groups + g) * dh, (groups + g + 1) * dh)
        lo = slice(2 * g * dh, (2 * g + 1) * dh)
        hi = slice((2 * g + 1) * dh, (2 * g + 2) * dh)
        kc = _rope(_rms(cmp_ref[:, sl].astype(F32), kn_ref[0:1, :]), c, s1, s2)
        kcv_ref[g] = kc.astype(BF16)
        kcv_ref[groups + g] = cmp_ref[:, vl]
        ks = _rope(_rms(sel_ref[:, sl].astype(F32), kn_ref[1:2, :]), c, s1, s2)
        ksa_ref[:, lo] = ks.astype(BF16)
        ksa_ref[:, hi] = onehot
        vsa_ref[:, lo] = sel_ref[:, vl]
        vsa_ref[:, hi] = ones
        kw = _rope(_rms(win_ref[:, sl].astype(F32), kn_ref[2:3, :]), c, s1, s2)
        kw_ref[:, sl] = kw.astype(BF16)
        vwa_ref[:, lo] = win_ref[:, vl]
        vwa_ref[:, hi] = ones


def nsa_prep(proj, rope_c, rope_s1, rope_s2, q_norm, k_norm, *, d_model, tt=256):
    b, t, _ = proj.shape
    g, dh = NSA_KV_GROUPS, HEAD_DIM
    kv2 = 2 * g * dh
    base = d_model // kv2
    tab = pl.BlockSpec((None, tt, dh), lambda bi, i: (bi, i, 0))
    return pl.pallas_call(
        functools.partial(_nsa_prep_body, g), grid=(b, t // tt),
        in_specs=[pl.BlockSpec((None, tt, d_model), lambda bi, i: (bi, i, 0)),
                  pl.BlockSpec((None, tt, kv2), lambda bi, i: (bi, i, base)),
                  pl.BlockSpec((None, tt, kv2), lambda bi, i: (bi, i, base + 1)),
                  pl.BlockSpec((None, tt, kv2), lambda bi, i: (bi, i, base + 2)),
                  tab, tab, tab,
                  pl.BlockSpec((1, dh), lambda bi, i: (0, 0)),
                  pl.BlockSpec((3, dh), lambda bi, i: (0, 0))],
        out_specs=[pl.BlockSpec((None, d_model // dh, tt, dh), lambda bi, i: (bi, 0, i, 0)),
                   pl.BlockSpec((None, 2 * g, tt, dh), lambda bi, i: (bi, 0, i, 0)),
                   pl.BlockSpec((None, tt, 2 * g * dh), lambda bi, i: (bi, i, 0)),
                   pl.BlockSpec((None, tt, 2 * g * dh), lambda bi, i: (bi, i, 0)),
                   pl.BlockSpec((None, tt, g * dh), lambda bi, i: (bi, i, 0)),
                   pl.BlockSpec((None, tt, 2 * g * dh), lambda bi, i: (bi, i, 0))],
        out_shape=[jax.ShapeDtypeStruct((b, d_model // dh, t, dh), BF16),
                   jax.ShapeDtypeStruct((b, 2 * g, t, dh), BF16),
                   jax.ShapeDtypeStruct((b, t, 2 * g * dh), BF16),
                   jax.ShapeDtypeStruct((b, t, 2 * g * dh), BF16),
                   jax.ShapeDtypeStruct((b, t, g * dh), BF16),
                   jax.ShapeDtypeStruct((b, t, 2 * g * dh), BF16)],
        compiler_params=_cparams("parallel", "parallel"), name="nsa_prep",
    )(proj, proj, proj, proj, rope_c, rope_s1, rope_s2, q_norm.reshape(1, dh), k_norm)


def _nsa_compress_body(p_ref, w1_ref, w2_ref, pos_ref, o_ref):
    n_pieces, half = p_ref.shape
    p = p_ref[...]
    u = _dot(p, w1_ref[0:half, :])
    v = _dot(p, w1_ref[half:2 * half, :])
    pos = jnp.broadcast_to(pos_ref[...], (8, 2 * half)).astype(BF16)
    c = _dot(pos, w1_ref[...])[0:1, :]
    hid = u + pltpu.roll(v, n_pieces - 1, 0) + c
    hid = hid * jax.nn.sigmoid(hid)
    o_ref[...] = _dot(hid.astype(BF16), w2_ref[...]).astype(o_ref.dtype)


def nsa_compress(kcv, w1, w2, pos):
    b, g2, n_pieces, half = kcv.shape
    dh = HEAD_DIM
    g = g2 // 2
    return pl.pallas_call(
        _nsa_compress_body, grid=(b, g2),
        in_specs=[pl.BlockSpec((None, None, n_pieces, half), lambda bi, gi: (bi, gi, 0, 0)),
                  pl.BlockSpec((None, 2 * half, dh), lambda bi, gi: (gi // g, 0, 0)),
                  pl.BlockSpec((None, dh, dh), lambda bi, gi: (gi // g, 0, 0)),
                  pl.BlockSpec((None, 1, 2 * half), lambda bi, gi: (gi // g, 0, 0))],
        out_specs=pl.BlockSpec((None, None, n_pieces, dh), lambda bi, gi: (bi, gi, 0, 0)),
        out_shape=jax.ShapeDtypeStruct((b, g2, n_pieces, dh), BF16),
        compiler_params=_cparams("parallel", "parallel"), name="nsa_compress",
    )(kcv, w1, w2, pos)


def _nsa_attn_body(tk, q_ref, gate_ref, kc_ref, vc_ref, ksa_ref, vsa_ref, kw_ref, vwa_ref, o_ref, mx_ref, acc_ref, s_ref):
    dh = HEAD_DIM
    hpg, tq, _ = q_ref.shape
    rows = hpg * tq
    n_cmp = kc_ref.shape[0]
    q0 = pl.program_id(2) * tq
    q_all = q_ref[...].reshape(rows, dh)
    t_one = q0 + lax.broadcasted_iota(jnp.int32, (tq, 1), 0)
    t_col = q0 + (lax.broadcasted_iota(jnp.int32, (rows, 1), 0) & (tq - 1))

    n_row = lax.broadcasted_iota(jnp.int32, (1, n_cmp), 1)
    cbias = jnp.where((n_row * CMP_STRIDE + (CMP_BLOCK - 1)) <= t_one, 0.0, -BIG)
    s_cmp = (_dot_nt(q_all, kc_ref[...]).reshape(hpg, tq, n_cmp) + cbias[None]).reshape(rows, n_cmp)
    p_cmp = jnp.exp(s_cmp - jnp.max(s_cmp, axis=-1, keepdims=True))
    inv = jnp.where(t_col >= CMP_BLOCK - 1, 1.0 / jnp.sum(p_cmp, axis=-1, keepdims=True), 0.0)
    p_cmp = p_cmp * inv
    o_cmp = _dot(p_cmp.astype(BF16), vc_ref[...])
    imp = jnp.sum(p_cmp.reshape(hpg, tq, n_cmp), axis=0)

    rs = SEL_BLOCK // CMP_STRIDE
    n_sel = ksa_ref.shape[0] // SEL_BLOCK
    mm = lax.broadcasted_iota(jnp.int32, (n_sel, n_cmp), 0)
    nn = lax.broadcasted_iota(jnp.int32, (n_sel, n_cmp), 1)
    pool_t = jnp.where((nn >= rs * mm - (CMP_BLOCK // CMP_STRIDE - 1)) & (nn <= rs * mm + rs - 1), 1.0, 0.0).astype(BF16)
    imp_hi = imp.astype(BF16)
    imp_lo = (imp - imp_hi.astype(F32)).astype(BF16)
    imp_sel = _dot_nt(pool_t, imp_hi) + _dot_nt(pool_t, imp_lo)

    blk = lax.broadcasted_iota(jnp.int32, (n_sel, tq), 0)
    cur = _floordiv_pow2(q0 + lax.broadcasted_iota(jnp.int32, (n_sel, tq), 1), SEL_BLOCK)
    forced = (blk == 0) | (blk == cur) | (blk == cur - 1)
    valid = blk <= cur
    score = jnp.where(valid, jnp.where(forced, BIG, imp_sel), -BIG)
    sub = 8
    groups = [score[g * sub:(g + 1) * sub, :] for g in range(n_sel // sub)]
    ranks = [jnp.zeros((sub, tq), F32) for _ in groups]
    sub_iota = lax.broadcasted_iota(jnp.int32, (sub, tq), 0)
    for m in range(n_sel):
        row = jnp.broadcast_to(score[m:m + 1, :], (sub, tq))
        for g, sg in enumerate(groups):
            if g * sub > m:
                inc = jnp.where(row >= sg, 1.0, 0.0)
            elif g * sub + sub - 1 <= m:
                inc = jnp.where(row > sg, 1.0, 0.0)
            else:
                inc = jnp.where(sub_iota + g * sub > m, jnp.where(row >= sg, 1.0, 0.0), jnp.where(row > sg, 1.0, 0.0))
            ranks[g] = ranks[g] + inc
    rank = jnp.concatenate(ranks, axis=0)
    selected = valid & (rank < float(min(SEL_TOPK, n_sel)))
    bias_t = jnp.where(selected, 0.0, SEL_MASK_BIAS)
    if n_sel < LANES:
        bias_t = jnp.concatenate([bias_t, jnp.zeros((LANES - n_sel, tq), F32)], axis=0)
    sel_bias = bias_t.T.astype(BF16)

    n_split = 2
    hr = rows // n_split
    q_parts = [q_all[i * hr:(i + 1) * hr] for i in range(n_split)]
    n_full = q0 // tk
    d_start = pl.multiple_of(n_full * tk, tk)

    def scores(ks):
        k = ksa_ref[pl.ds(ks, tk), :]
        bias = _dot_nt(sel_bias, k[:, dh:2 * dh])
        return [(_dot_nt(qp, k[:, 0:dh]).reshape(hpg // n_split, tq, tk) + bias[None]).reshape(hr, tk) for qp in q_parts]

    def fold(s):
        out = s[:, 0:LANES]
        for c in range(1, tk // LANES):
            out = jnp.maximum(out, s[:, c * LANES:(c + 1) * LANES])
        return out

    dmask = (d_start + lax.broadcasted_iota(jnp.int32, (1, tk), 1)) <= t_col[0:hr]
    for i, s in enumerate(scores(d_start)):
        r = slice(i * hr, (i + 1) * hr)
        s = jnp.where(dmask, s, -BIG)
        s_ref[n_full, r, :] = s
        mx_ref[r, :] = fold(s)

    def max_tiles(kts):
        all_s = [scores(pl.multiple_of(kt * tk, tk)) for kt in kts]
        for kt, parts in zip(kts, all_s):
            for i, s in enumerate(parts):
                r = slice(i * hr, (i + 1) * hr)
                s_ref[kt, r, :] = s
                mx_ref[r, :] = jnp.maximum(mx_ref[r, :], fold(s))

    def max_step(r, carry):
        max_tiles([2 * r, 2 * r + 1])
        return carry

    lax.fori_loop(0, n_full // 2, max_step, 0)

    @pl.when((n_full & 1) == 1)
    def _():
        max_tiles([n_full - 1])

    m_row = jnp.broadcast_to(jnp.max(mx_ref[...], axis=-1, keepdims=True), (rows, LANES))
    m_row = jnp.concatenate([m_row] * (tk // LANES), axis=-1)
    m_parts = [m_row[i * hr:(i + 1) * hr] for i in range(n_split)]

    acc_ref[...] = jnp.zeros_like(acc_ref)

    def acc_tiles(kts):
        ps = [[jnp.exp(s_ref[kt, i * hr:(i + 1) * hr, :] - m_parts[i]).astype(BF16) for i in range(n_split)] for kt in kts]
        for kt, parts in zip(kts, ps):
            v = vsa_ref[pl.ds(pl.multiple_of(kt * tk, tk), tk), :]
            for i, p in enumerate(parts):
                r = slice(i * hr, (i + 1) * hr)
                acc_ref[r, :] += _dot(p, v)

    def acc_step(r, carry):
        acc_tiles([2 * r, 2 * r + 1])
        return carry

    n_tiles = n_full + 1
    lax.fori_loop(0, n_tiles // 2, acc_step, 0)

    @pl.when((n_tiles & 1) == 1)
    def _():
        acc_tiles([n_tiles - 1])

    acc = acc_ref[...]
    o_sel = acc[:, 0:dh] / acc[:, dh:2 * dh]

    w_len = WINDOW + tq
    w_start = pl.multiple_of(jnp.maximum(q0 - WINDOW, 0), tq)
    diff = t_one - (w_start + lax.broadcasted_iota(jnp.int32, (1, w_len), 1))
    wbias = jnp.where((diff >= 0) & (diff < WINDOW), 0.0, -BIG)
    s_win = (_dot_nt(q_all, kw_ref[pl.ds(w_start, w_len), :]).reshape(hpg, tq, w_len) + wbias[None]).reshape(rows, w_len)
    p_win = jnp.exp(s_win - jnp.max(s_win, axis=-1, keepdims=True))
    acc_w = _dot(p_win.astype(BF16), vwa_ref[pl.ds(w_start, w_len), :])
    o_win = acc_w[:, 0:dh] / acc_w[:, dh:2 * dh]

    for j in range(hpg):
        r = slice(j * tq, (j + 1) * tq)
        g_cmp = gate_ref[:, j:j + 1]
        g_sel = gate_ref[:, hpg + j:hpg + j + 1]
        g_win = gate_ref[:, 2 * hpg + j:2 * hpg + j + 1]
        o_ref[:, j * dh:(j + 1) * dh] = (g_cmp * o_cmp[r] + g_sel * o_sel[r] + g_win * o_win[r]).astype(o_ref.dtype)


def nsa_attention(qh, gates, kv_cmp, ks_aug, vs_aug, kw, vw_aug, *, d_model, tq=128, tk=256):
    b, _, t, dh = qh.shape
    g = NSA_KV_GROUPS
    hpg = d_model // dh // g
    n_cmp = kv_cmp.shape[2]
    aug = pl.BlockSpec((None, t, 2 * dh), lambda bi, gi, i: (bi, 0, gi))
    return pl.pallas_call(
        functools.partial(_nsa_attn_body, tk), grid=(b, g, t // tq),
        in_specs=[pl.BlockSpec((None, hpg, tq, dh), lambda bi, gi, i: (bi, gi, i, 0)),
                  pl.BlockSpec((None, tq, LANES), lambda bi, gi, i: (bi, i, gi)),
                  pl.BlockSpec((None, None, n_cmp, dh), lambda bi, gi, i: (bi, gi, 0, 0)),
                  pl.BlockSpec((None, None, n_cmp, dh), lambda bi, gi, i: (bi, g + gi, 0, 0)),
                  aug, aug,
                  pl.BlockSpec((None, t, dh), lambda bi, gi, i: (bi, 0, gi)),
                  aug],
        out_specs=pl.BlockSpec((None, tq, hpg * dh), lambda bi, gi, i: (bi, i, gi)),
        out_shape=jax.ShapeDtypeStruct((b, t, d_model), BF16),
        scratch_shapes=[pltpu.VMEM((hpg * tq, LANES), F32), pltpu.VMEM((hpg * tq, 2 * dh), F32),
                        pltpu.VMEM((t // tk, hpg * tq, tk), F32)],
        compiler_params=_cparams("parallel", "parallel", "arbitrary"), name="nsa_attention",
    )(qh, gates, kv_cmp, kv_cmp, ks_aug, vs_aug, kw, vw_aug)


def nsa_mixer(proj, gates, rope_tabs, q_norm, k_norm, cmp_pos, cmp_w1, cmp_w2, *, d_model):
    b, t, _ = proj.shape
    dh = HEAD_DIM
    qh, kcv, ks_aug, vs_aug, kw, vw_aug = nsa_prep(proj, *rope_tabs, q_norm, k_norm, d_model=d_model)
    n_pieces = t // CMP_STRIDE
    kcv = kcv.reshape(b, 2 * NSA_KV_GROUPS, n_pieces, CMP_STRIDE * dh)
    kv_cmp = nsa_compress(kcv, cmp_w1.astype(BF16), cmp_w2.astype(BF16), cmp_pos.reshape(2, 1, CMP_BLOCK * dh))
    return nsa_attention(qh, gates.reshape(b, t, NSA_KV_GROUPS * LANES), kv_cmp, ks_aug, vs_aug, kw, vw_aug, d_model=d_model)


def _conv_body(halo, cw, p_ref, ph_ref, bin_ref, w_ref, wb_ref, lg_ref, lb_ref, o_ref, u_ref, y_ref, sh_ref):
    tt, d = o_ref.shape
    i = pl.program_id(1)

    def glu(ref):
        a = ref[:, 0:d].astype(F32) + bin_ref[:, 0:d]
        gate = ref[:, d:2 * d].astype(F32) + bin_ref[:, d:2 * d]
        return a * jax.nn.sigmoid(gate)

    u_ref[halo:halo + tt, :] = glu(p_ref)

    @pl.when(i == 0)
    def _():
        u_ref[0:halo, :] = jnp.zeros((halo, d), F32)

    @pl.when(i > 0)
    def _():
        u_ref[0:halo, :] = glu(ph_ref)

    off = halo - (CONV_WIDTH - 1)
    for c0 in range(0, d, cw):
        cols = slice(c0, c0 + cw)
        for c in range(SUBLANES):
            rows_c = tt + SUBLANES * ((CONV_WIDTH - 1 - c) // SUBLANES)
            sh_ref[c, 0:rows_c, :] = u_ref[off + c:off + c + rows_c, cols]
        acc = jnp.zeros((tt, cw), F32) + wb_ref[:, cols]
        for k in range(CONV_WIDTH):
            b, c = divmod(k, SUBLANES)
            acc = acc + w_ref[k:k + 1, cols] * sh_ref[c, SUBLANES * b:SUBLANES * b + tt, :]
        y_ref[:, cols] = acc

    y = y_ref[...]
    yc = y - jnp.mean(y, axis=-1, keepdims=True)
    yn = yc * lax.rsqrt(jnp.mean(yc * yc, axis=-1, keepdims=True) + NORM_EPS) * lg_ref[...] + lb_ref[...]
    o_ref[...] = (yn * jax.nn.sigmoid(yn)).astype(o_ref.dtype)


def conv_mixer(proj, b_in, dw_w, dw_b, ln_g, ln_b, *, d_model, tt=256, halo=32, cw=256):
    b, t, _ = proj.shape
    d = d_model
    hb = tt // halo
    row = lambda v: v.reshape(1, -1)
    const = lambda shape: pl.BlockSpec(shape, lambda bi, i: (0, 0))
    return pl.pallas_call(
        functools.partial(_conv_body, halo, cw), grid=(b, t // tt),
        in_specs=[pl.BlockSpec((None, tt, 2 * d), lambda bi, i: (bi, i, 0)),
                  pl.BlockSpec((None, halo, 2 * d), lambda bi, i: (bi, jnp.maximum(i * hb - 1, 0), 0)),
                  const((1, 2 * d)), const((CONV_WIDTH, d)), const((1, d)), const((1, d)), const((1, d))],
        out_specs=pl.BlockSpec((None, tt, d), lambda bi, i: (bi, i, 0)),
        out_shape=jax.ShapeDtypeStruct((b, t, d), BF16),
        scratch_shapes=[pltpu.VMEM((halo + tt, d), F32), pltpu.VMEM((tt, d), F32),
                        pltpu.VMEM((SUBLANES, tt + SUBLANES * ((CONV_WIDTH - 1) // SUBLANES), cw), F32)],
        compiler_params=_cparams("parallel", "arbitrary"), name="conv_mixer",
    )(proj, proj, row(b_in), dw_w, row(dw_b), row(ln_g), row(ln_b))


def _sb_body(tk, q_ref, k_ref, v_ref, o_ref):
    tq = q_ref.shape[0]
    dh = HEAD_DIM
    hp = q_ref.shape[1] // dh
    heads = [slice(h * dh, (h + 1) * dh) for h in range(hp)]
    q0 = pl.program_id(2) * tq
    ss = lax.broadcasted_iota(jnp.int32, (tk, tk), 0)
    jj = lax.broadcasted_iota(jnp.int32, (tk, tk), 1)
    suffix = jnp.where(ss > jj, 1.0, 0.0).astype(BF16)

    def tiles(ks, carries, strict):
        zs = [_dot_nt(q_ref[:, hs], k_ref[pl.ds(ks, tk), hs]) for hs in heads]
        log_betas, log_keeps = [], []
        for z in zs:
            neg_abs = lax.bitcast_convert_type(lax.bitcast_convert_type(z, jnp.uint32) | jnp.uint32(0x80000000), F32)
            log_beta = jnp.minimum(z, 0.0) - jnp.log2(1.0 + jnp.exp2(neg_abs))
            log_keep = log_beta - z
            if strict is not None:
                log_keep = jnp.where(strict, log_keep, 0.0)
            log_betas.append(log_beta)
            log_keeps.append(log_keep)
        afters = [_dot(lk.astype(BF16), suffix) for lk in log_keeps]
        out = []
        for h in range(hp):
            c_run, acc = carries[h]
            a = jnp.exp2(log_betas[h] + afters[h] + c_run)
            if strict is not None:
                a = jnp.where(strict, a, 0.0)
            acc = acc + _dot(a.astype(BF16), v_ref[pl.ds(ks, tk), heads[h]])
            c_run = c_run + jnp.sum(log_keeps[h], axis=-1, keepdims=True)
            out.append((c_run, acc))
        return tuple(out)

    carries = tuple((jnp.zeros((tq, 1), F32), jnp.zeros((tq, dh), F32)) for _ in range(hp))
    t_col = q0 + lax.broadcasted_iota(jnp.int32, (tq, 1), 0)
    for d in range(tq // tk - 1, -1, -1):
        ks = pl.multiple_of(q0 + d * tk, tk)
        strict = (ks + lax.broadcasted_iota(jnp.int32, (1, tk), 1)) < t_col
        carries = tiles(ks, carries, strict)
    n_full = q0 // tk

    def step(r, carries):
        return tiles(pl.multiple_of((n_full - 1 - r) * tk, tk), carries, None)

    carries = lax.fori_loop(0, n_full, step, carries)
    for h in range(hp):
        o_ref[:, heads[h]] = carries[h][1].astype(o_ref.dtype)


def sb_mixer(proj, *, d_model, tq=256, tk=256, hp=4):
    b, t, _ = proj.shape
    w = hp * HEAD_DIM
    nb = d_model // w
    return pl.pallas_call(
        functools.partial(_sb_body, tk), grid=(b, nb, t // tq),
        in_specs=[pl.BlockSpec((None, tq, w), lambda bi, h, i: (bi, i, h)),
                  pl.BlockSpec((None, t, w), lambda bi, h, i: (bi, 0, nb + h)),
                  pl.BlockSpec((None, t, w), lambda bi, h, i: (bi, 0, 2 * nb + h))],
        out_specs=pl.BlockSpec((None, tq, w), lambda bi, h, i: (bi, i, h)),
        out_shape=jax.ShapeDtypeStruct((b, t, d_model), BF16),
        compiler_params=_cparams("parallel", "parallel", "arbitrary"), name="sb_mixer",
    )(proj, proj, proj)


def _gelu(x):
    return 0.5 * x * (1.0 + lax.erf(x * (2.0 ** -0.5)))


def _gmlp_body(p_ref, lg_ref, lb_ref, ws_ref, bs_ref, o_ref):
    tt, d = o_ref.shape
    gc = ws_ref.shape[1]
    v = _gelu(p_ref[:, d:2 * d].astype(F32))
    vc = v - jnp.mean(v, axis=-1, keepdims=True)
    vn = (vc * lax.rsqrt(jnp.mean(vc * vc, axis=-1, keepdims=True) + NORM_EPS) * lg_ref[...] + lb_ref[...]).astype(BF16)
    causal = lax.broadcasted_iota(jnp.int32, (gc, gc), 0) >= lax.broadcasted_iota(jnp.int32, (gc, gc), 1)
    for g in range(d // gc):
        sl = slice(g * gc, (g + 1) * gc)
        ws = jnp.where(causal, ws_ref[g], 0.0).astype(BF16)
        mixed = _dot(ws, vn[:, sl]) + bs_ref[:, sl]
        o_ref[:, sl] = (_gelu(p_ref[:, sl].astype(F32)) * mixed).astype(o_ref.dtype)


def gmlp_mixer(proj, ln_g, ln_b, w_s, b_s, *, d_model):
    b, t, _ = proj.shape
    d = d_model
    n_groups, gc, _ = w_s.shape
    row = lambda v: v.reshape(1, -1)
    const = lambda shape: pl.BlockSpec(shape, lambda bi, i: (0,) * len(shape))
    bs_cols = jnp.repeat(b_s.T, d // n_groups, axis=1)
    return pl.pallas_call(
        _gmlp_body, grid=(b, t // GM_CHUNK),
        in_specs=[pl.BlockSpec((None, GM_CHUNK, 2 * d), lambda bi, i: (bi, i, 0)),
                  const((1, d)), const((1, d)), const((n_groups, gc, gc)), const((GM_CHUNK, d))],
        out_specs=pl.BlockSpec((None, GM_CHUNK, d), lambda bi, i: (bi, i, 0)),
        out_shape=jax.ShapeDtypeStruct((b, t, d), BF16),
        compiler_params=_cparams("parallel", "parallel"), name="gmlp_mixer",
    )(proj, row(ln_g), row(ln_b), w_s, bs_cols)


def _rope_tables(positions):
    half = ROPE_DIM // 2
    inv = 1.0 / (ROPE_THETA ** (jnp.arange(0, ROPE_DIM, 2, dtype=F32) / ROPE_DIM))
    ang = positions.astype(F32)[..., None] * inv
    cos, sin = jnp.cos(ang), jnp.sin(ang)
    zeros = jnp.zeros_like(cos)
    tail = jnp.zeros(cos.shape[:-1] + (HEAD_DIM - ROPE_DIM,), F32)
    c = jnp.concatenate([cos, cos, tail + 1.0], axis=-1)
    s1 = jnp.concatenate([-sin, zeros, tail], axis=-1)
    s2 = jnp.concatenate([zeros, sin, tail], axis=-1)
    return c, s1, s2


def _pad_to(w, axis, mult):
    pad = (-w.shape[axis]) % mult
    if pad == 0:
        return w
    widths = [(0, 0)] * w.ndim
    widths[axis] = (0, pad)
    return jnp.pad(w, widths)


def _cast_pad_halves_body(n_in, w_ref, o_ref):
    n_out = o_ref.shape[1] // 2
    for half in range(2):
        o_ref[:, half * n_out:half * n_out + n_in] = w_ref[:, half * n_in:(half + 1) * n_in].astype(o_ref.dtype)
        if n_out > n_in:
            o_ref[:, half * n_out + n_in:(half + 1) * n_out] = jnp.zeros((o_ref.shape[0], n_out - n_in), o_ref.dtype)


def cast_pad_halves(w, n_in, mult, *, tr=64):
    layers, k, _ = w.shape
    n_out = n_in + (-n_in) % mult
    return pl.pallas_call(
        functools.partial(_cast_pad_halves_body, n_in), grid=(layers, k // tr),
        in_specs=[pl.BlockSpec((None, tr, 2 * n_in), lambda l, i: (l, i, 0))],
        out_specs=pl.BlockSpec((None, tr, 2 * n_out), lambda l, i: (l, i, 0)),
        out_shape=jax.ShapeDtypeStruct((layers, k, 2 * n_out), BF16),
        compiler_params=_cparams("parallel", "parallel"), name="cast_pad_halves")(w)


FFN_COL_TILE = 512


def _ffn(x2, norm_g, w_gu, w_down, layer):
    act = norm_swiglu(x2, norm_g, w_gu, layer, tn=FFN_COL_TILE)
    return matmul_residual(act, w_down, layer, x2, 0.5)


def kernel(x, mem, positions, ffn1_norm, ffn1_w_gu, ffn1_w_down, mix_norm, ffn2_norm, ffn2_w_gu, ffn2_w_down, mem_norm, mem_w_kv, mem_k_norm, mem_q_norm, nsa_w_in, nsa_q_norm, nsa_k_norm, nsa_cmp_pos, nsa_cmp_w1, nsa_cmp_w2, nsa_w_out, conv_w_in, conv_b_in, conv_dw_w, conv_dw_b, conv_ln_g, conv_ln_b, conv_w_out, sb_w_in, sb_w_out, gm_w_in, gm_ln_g, gm_ln_b, gm_ws, gm_bs, gm_w_out):
    b, t, d = x.shape
    depth = ffn1_norm.shape[0]
    d_ff = ffn1_w_down.shape[1]
    mem_len = mem.shape[1]
    mem_width = mem_w_kv.shape[1] // 2
    n_heads = d // HEAD_DIM
    n_mixers = 4

    rope_tabs = _rope_tables(positions)
    w_gu1 = cast_pad_halves(ffn1_w_gu, d_ff, FFN_COL_TILE)
    w_gu2 = cast_pad_halves(ffn2_w_gu, d_ff, FFN_COL_TILE)
    w_down1 = ffn1_w_down.astype(BF16)
    w_down2 = ffn2_w_down.astype(BF16)
    mkv = norm_matmul(mem.reshape(b * mem_len, d), mem_norm, mem_w_kv.astype(BF16)).reshape(b, mem_len, 2 * mem_width)

    x2 = x.reshape(b * t, d)
    for i in range(depth):
        kind, j = i % n_mixers, i // n_mixers
        x2 = _ffn(x2, ffn1_norm[i], w_gu1, w_down1, i)
        if kind == 0:
            w_in = nsa_w_in[j]
            n_main = w_in.shape[1] - mem_width - 3 * n_heads
            hpg = n_heads // NSA_KV_GROUPS
            w_gate = w_in[:, n_main:n_main + 3 * n_heads].reshape(d, NSA_KV_GROUPS, hpg, 3).transpose(0, 1, 3, 2)
            w_gate = _pad_to(w_gate.reshape(d, NSA_KV_GROUPS, 3 * hpg).astype(BF16), 2, LANES).reshape(d, -1)
            w_main = jnp.concatenate([w_in[:, :n_main], w_in[:, -mem_width:]], axis=1).astype(BF16)
            proj, gates = norm_matmul(x2, mix_norm[i], w_main, side_w=w_gate)
            proj = proj.reshape(b, t, -1)
            mixed = nsa_mixer(proj, gates, rope_tabs, nsa_q_norm[j], nsa_k_norm[j], nsa_cmp_pos[j],
                              nsa_cmp_w1[j], nsa_cmp_w2[j], d_model=d)
            w_out = nsa_w_out[j]
        elif kind == 1:
            proj = norm_matmul(x2, mix_norm[i], conv_w_in[j].astype(BF16)).reshape(b, t, -1)
            mixed = conv_mixer(proj, conv_b_in[j], conv_dw_w[j], conv_dw_b[j], conv_ln_g[j], conv_ln_b[j], d_model=d)
            w_out = conv_w_out[j]
        elif kind == 2:
            w_in = sb_w_in[j]
            w_in = jnp.concatenate([w_in[:, :d] * SB_SCORE_SCALE, w_in[:, d:]], axis=1).astype(BF16)
            proj = norm_matmul(x2, mix_norm[i], w_in).reshape(b, t, -1)
            mixed = sb_mixer(proj, d_model=d)
            w_out = sb_w_out[j]
        else:
            proj = norm_matmul(x2, mix_norm[i], gm_w_in[j].astype(BF16)).reshape(b, t, -1)
            mixed = gmlp_mixer(proj, gm_ln_g[j], gm_ln_b[j], gm_ws[j], gm_bs[j], d_model=d)
            w_out = gm_w_out[j]
        mem_out = mem_attention(proj, mkv, mem_q_norm[i], mem_k_norm, width=mem_width)
        w_out = w_out.astype(BF16)
        x2 = matmul2_residual(mixed.reshape(b * t, d), mem_out.reshape(b * t, mem_width), w_out[:d], w_out[d:], x2)
        x2 = _ffn(x2, ffn2_norm[i], w_gu2, w_down2, i)
    return x2.reshape(b, t, d)
```
